```python
import math
import jax, jax.numpy as jnp
from jax import lax
import numpy as np

D_MODEL = 1024
BATCH = 16
SEQ = 256
DEPTH = 1
DEC_BATCH = 4
DEC_SEQ = 1024
PAST_LEN = 512

GRID_W = 64
ATTN_HEADS = 8
ATTN_DH = 64
ATTN_VD = 2 * ATTN_DH
ROPE_BASE = 10000.0
ROPE_AXIS_DIM = ATTN_DH // 2
Q_BLOCK = 128
D_INNER = 2 * D_MODEL
SSM_HEADDIM = 64
SSM_HEADS = D_INNER // SSM_HEADDIM
SSM_GROUPS = 8
SSM_STATE = 128
D_CONV = 5
CHUNK = 128
CONV_DIM = D_INNER + 2 * SSM_GROUPS * SSM_STATE
D_FF = ((8 * D_MODEL // 3 + 255) // 256) * 256
W_Q = ATTN_HEADS * 2 * ATTN_DH
W_K = ATTN_HEADS * 2 * ATTN_DH
W_V = ATTN_HEADS * ATTN_VD
W_Z = D_INNER
W_XBC = CONV_DIM
W_DT = 2 * SSM_HEADS
W_GATES = 2 * D_MODEL
D_IN_TOTAL = W_Q + W_K + W_V + W_Z + W_XBC + W_DT + W_GATES
EPS = 1e-6

kernel_name = "hybrid_diffattn_ssd_prefix_context_step"


def rms_norm(x, g):
    xf = x.astype(jnp.float32)
    y = xf * lax.rsqrt(jnp.mean(xf * xf, axis=-1, keepdims=True) + EPS)
    return (y * g.astype(jnp.float32)).astype(x.dtype)


def axial_rope_tables(n_tok):
    rows = n_tok // GRID_W
    row = jnp.repeat(jnp.arange(rows, dtype=jnp.float32), GRID_W)
    col = jnp.tile(jnp.arange(GRID_W, dtype=jnp.float32), rows)
    inv = ROPE_BASE ** (-jnp.arange(0, ROPE_AXIS_DIM, 2, dtype=jnp.float32) / ROPE_AXIS_DIM)
    ang_r = row[:, None] * inv[None, :]
    ang_c = col[:, None] * inv[None, :]
    return jnp.cos(ang_r), jnp.sin(ang_r), jnp.cos(ang_c), jnp.sin(ang_c)


def _rotate(x, cos, sin):
    x1, x2 = jnp.split(x, 2, axis=-1)
    cos = cos[None, :, None, None, :]
    sin = sin[None, :, None, None, :]
    return jnp.concatenate([x1 * cos - x2 * sin, x2 * cos + x1 * sin], axis=-1)


def apply_axial_rope(x, rope):
    cos_r, sin_r, cos_c, sin_c = rope
    xf = x.astype(jnp.float32)
    xr = _rotate(xf[..., :ROPE_AXIS_DIM], cos_r, sin_r)
    xc = _rotate(xf[..., ROPE_AXIS_DIM:], cos_c, sin_c)
    return jnp.concatenate([xr, xc], axis=-1).astype(x.dtype)


def diff_attention(q, k, v, lam):
    b, lq, h, _, dh = q.shape
    nb = lq // Q_BLOCK
    qb = jnp.moveaxis(q.reshape(b, nb, Q_BLOCK, h, 2, dh), 1, 0)
    scale = dh ** -0.5

    def one_block(qblk):
        s = jnp.einsum("bqhcd,bkhcd->bhcqk", qblk, k).astype(jnp.float32) * scale
        p = jax.nn.softmax(s, axis=-1)
        w = (p[:, :, 0] - lam * p[:, :, 1]).astype(v.dtype)
        return jnp.einsum("bhqk,bkhe->bqhe", w, v)

    o = lax.map(one_block, qb)
    return jnp.moveaxis(o, 0, 1).reshape(b, lq, h, v.shape[-1])


def centred_dwconv(x, w, bias):
    y = lax.conv_general_dilated(
        x, w[:, None, :].astype(x.dtype), window_strides=(1,),
        padding=[(D_CONV // 2, D_CONV // 2)],
        dimension_numbers=("NWC", "WIO", "NWC"),
        feature_group_count=x.shape[-1])
    return y + bias


def _segsum(a):
    t = a.shape[-1]
    cs = jnp.cumsum(a, axis=-1)
    diff = cs[..., :, None] - cs[..., None, :]
    mask = jnp.tril(jnp.ones((t, t), dtype=bool))
    return jnp.where(mask, diff, -jnp.inf)


def ssd_scan(x, dt, A, Bm, Cm, h0):
    b, L, H, P = x.shape
    G, N = Bm.shape[2], Bm.shape[3]
    R = H // G
    nc = L // CHUNK
    x = x.astype(jnp.float32)
    xd = (x * dt[..., None]).reshape(b, nc, CHUNK, G, R, P)
    a = (dt * A).reshape(b, nc, CHUNK, G, R).transpose(0, 3, 4, 1, 2)
    Bc = Bm.astype(jnp.float32).reshape(b, nc, CHUNK, G, N)
    Cc = Cm.astype(jnp.float32).reshape(b, nc, CHUNK, G, N)
    a_cum = jnp.cumsum(a, axis=-1)
    Lmat = jnp.exp(_segsum(a))
    CB = jnp.einsum("bclgn,bcsgn->bgcls", Cc, Bc)
    y_diag = jnp.einsum("bgcls,bgrcls,bcsgrp->bclgrp", CB, Lmat, xd)
    decay_states = jnp.exp(a_cum[..., -1:] - a_cum)
    states = jnp.einsum("bclgn,bgrcl,bclgrp->bcgrpn", Bc, decay_states, xd)
    states = jnp.concatenate([h0.astype(jnp.float32).reshape(b, 1, G, R, P, N), states], axis=1)
    chunk_a = jnp.pad(a_cum[..., -1], ((0, 0), (0, 0), (0, 0), (1, 0)))
    decay_chunk = jnp.exp(_segsum(chunk_a))
    new_states = jnp.einsum("bgrzc,bcgrpn->bzgrpn", decay_chunk, states)
    prev_states, final = new_states[:, :-1], new_states[:, -1]
    y_off = jnp.einsum("bclgn,bcgrpn,bgrcl->bclgrp", Cc, prev_states, jnp.exp(a_cum))
    y = (y_diag + y_off).reshape(b, L, H, P)
    return y, final.reshape(b, H, P, N)


def ada_mod(cond, w_ada, b_ada):
    return (jax.nn.silu(cond) @ w_ada + b_ada)[:, None, :]


def trunk_layer(x, mod, rope, ctx_k, ctx_v, h0_f, h0_b, lam_init,
                norm1_g, norm2_g, w_in, q_norm_g, k_norm_g,
                lambda_q1, lambda_k1, lambda_q2, lambda_k2, attn_sub_g,
                conv_w, conv_b, A_log, dt_bias, D_skip, ssm_norm_g,
                w_branch_a, w_branch_b, w_out, w_ffn_gate, w_ffn_up, w_ffn_down):
    b, L, _ = x.shape
    shift1, scale1, gate1, shift2, scale2, gate2 = jnp.split(mod, 6, axis=-1)
    h = rms_norm(x, norm1_g) * (1 + scale1) + shift1
    splits = [W_Q, W_Q + W_K, W_Q + W_K + W_V, W_Q + W_K + W_V + W_Z,
              W_Q + W_K + W_V + W_Z + W_XBC, W_Q + W_K + W_V + W_Z + W_XBC + W_DT]
    q, k, v, z, xbc, dt_raw, gates = jnp.split(h @ w_in, splits, axis=-1)

    q = rms_norm(q.reshape(b, L, ATTN_HEADS, 2, ATTN_DH), q_norm_g)
    k = rms_norm(k.reshape(b, L, ATTN_HEADS, 2, ATTN_DH), k_norm_g)
    v = v.reshape(b, L, ATTN_HEADS, ATTN_VD)
    if rope is None:
        keys, vals = k, v
    else:
        q = apply_axial_rope(q, rope)
        keys = jnp.concatenate([apply_axial_rope(k, rope), ctx_k.astype(k.dtype)], axis=1)
        vals = jnp.concatenate([v, ctx_v.astype(v.dtype)], axis=1)
    lam = (jnp.exp(jnp.sum(lambda_q1.astype(jnp.float32) * lambda_k1.astype(jnp.float32)))
           - jnp.exp(jnp.sum(lambda_q2.astype(jnp.float32) * lambda_k2.astype(jnp.float32)))
           + lam_init)
    o = diff_attention(q, keys, vals, lam)
    o = rms_norm(o, attn_sub_g) * (1.0 - lam_init)
    out_a = o.reshape(b, L, ATTN_HEADS * ATTN_VD) @ w_branch_a

    xbc = jax.nn.silu(centred_dwconv(xbc, conv_w, conv_b))
    xs, Bs, Cs = jnp.split(xbc, [D_INNER, D_INNER + SSM_GROUPS * SSM_STATE], axis=-1)
    xs = xs.reshape(b, L, SSM_HEADS, SSM_HEADDIM)
    Bs = Bs.reshape(b, L, SSM_GROUPS, SSM_STATE)
    Cs = Cs.reshape(b, L, SSM_GROUPS, SSM_STATE)
    dt = jax.nn.softplus(dt_raw.astype(jnp.float32).reshape(b, L, 2, SSM_HEADS)
                         + dt_bias.astype(jnp.float32))
    A = -jnp.exp(A_log.astype(jnp.float32))
    if h0_f is None:
        h0_f = jnp.zeros((b, SSM_HEADS, SSM_HEADDIM, SSM_STATE), jnp.float32)
        h0_b = jnp.zeros((b, SSM_HEADS, SSM_HEADDIM, SSM_STATE), jnp.float32)
    y_f, hf = ssd_scan(xs, dt[:, :, 0], A[0], Bs, Cs, h0_f)
    y_b, hb = ssd_scan(jnp.flip(xs, 1), jnp.flip(dt[:, :, 1], 1), A[1],
                       jnp.flip(Bs, 1), jnp.flip(Cs, 1), h0_b)
    y = y_f + jnp.flip(y_b, 1) + D_skip.astype(jnp.float32)[:, None] * xs.astype(jnp.float32)
    y = rms_norm(y.reshape(b, L, D_INNER).astype(x.dtype) * jax.nn.silu(z), ssm_norm_g)
    out_b = y @ w_branch_b

    g_a, g_b = jnp.split(gates, 2, axis=-1)
    merged = jax.nn.sigmoid(g_a) * out_a + jax.nn.sigmoid(g_b) * out_b
    x = x + gate1 * (merged @ w_out)

    h2 = rms_norm(x, norm2_g) * (1 + scale2) + shift2
    x = x + gate2 * ((jax.nn.silu(h2 @ w_ffn_gate) * (h2 @ w_ffn_up)) @ w_ffn_down)
    return x, k, v, hf.astype(x.dtype), hb.astype(x.dtype)


def setup_inputs(seed: int = 0) -> dict:
    key = jax.random.key(seed)
    ks = jax.random.split(key, 40)
    f32 = jnp.float32

    def nrm(i, shape, scale):
        return jax.random.normal(ks[i], shape, f32) * scale

    dt0 = jnp.exp(jax.random.uniform(ks[30], (DEPTH, 2, SSM_HEADS), f32,
                                     math.log(1e-3), math.log(1e-1)))
    return {
        "x_prompt": nrm(0, (BATCH, SEQ, D_MODEL), 1.0),
        "x_sample": nrm(1, (DEC_BATCH, DEC_SEQ, D_MODEL), 1.0),
        "c": nrm(2, (DEC_BATCH, D_MODEL), 1.0),
        "cache_k": nrm(3, (DEC_BATCH, DEPTH, PAST_LEN, ATTN_HEADS, 2, ATTN_DH), 1.0),
        "cache_v": nrm(4, (DEC_BATCH, DEPTH, PAST_LEN, ATTN_HEADS, ATTN_VD), 1.0),
        "state_ssm_fwd": nrm(5, (DEC_BATCH, DEPTH, SSM_HEADS, SSM_HEADDIM, SSM_STATE), 0.1),
        "state_ssm_bwd": nrm(6, (DEC_BATCH, DEPTH, SSM_HEADS, SSM_HEADDIM, SSM_STATE), 0.1),
        "c_ctx": nrm(7, (D_MODEL,), 1.0),
        "norm1_g": 1.0 + nrm(8, (DEPTH, D_MODEL), 0.05),
        "norm2_g": 1.0 + nrm(9, (DEPTH, D_MODEL), 0.05),
        "w_ada": nrm(10, (DEPTH, D_MODEL, 6 * D_MODEL), 0.5 * D_MODEL ** -0.5),
        "b_ada": nrm(11, (DEPTH, 6 * D_MODEL), 0.02),
        "w_in": nrm(12, (DEPTH, D_MODEL, D_IN_TOTAL), D_MODEL ** -0.5),
        "q_norm_g": 1.0 + nrm(13, (DEPTH, ATTN_DH), 0.05),
        "k_norm_g": 1.0 + nrm(14, (DEPTH, ATTN_DH), 0.05),
        "lambda_q1": nrm(15, (DEPTH, ATTN_DH), 0.1),
        "lambda_k1": nrm(16, (DEPTH, ATTN_DH), 0.1),
        "lambda_q2": nrm(17, (DEPTH, ATTN_DH), 0.1),
        "lambda_k2": nrm(18, (DEPTH, ATTN_DH), 0.1),
        "attn_sub_g": 1.0 + nrm(19, (DEPTH, ATTN_VD), 0.05),
        "conv_w": nrm(20, (DEPTH, D_CONV, CONV_DIM), D_CONV ** -0.5),
        "conv_b": nrm(21, (DEPTH, CONV_DIM), 0.02),
        "A_log": jnp.log(jax.random.uniform(ks[22], (DEPTH, 2, SSM_HEADS), f32, 1.0, 16.0)),
        "dt_bias": dt0 + jnp.log(-jnp.expm1(-dt0)),
        "D_skip": 1.0 + nrm(23, (DEPTH, SSM_HEADS), 0.05),
        "ssm_norm_g": 1.0 + nrm(24, (DEPTH, D_INNER), 0.05),
        "w_branch_a": nrm(25, (DEPTH, ATTN_HEADS * ATTN_VD, D_MODEL), (ATTN_HEADS * ATTN_VD) ** -0.5),
        "w_branch_b": nrm(26, (DEPTH, D_INNER, D_MODEL), D_INNER ** -0.5),
        "w_out": nrm(27, (DEPTH, D_MODEL, D_MODEL), D_MODEL ** -0.5),
        "w_ffn_gate": nrm(28, (DEPTH, D_MODEL, D_FF), D_MODEL ** -0.5),
        "w_ffn_up": nrm(29, (DEPTH, D_MODEL, D_FF), D_MODEL ** -0.5),
        "w_ffn_down": nrm(31, (DEPTH, D_FF, D_MODEL), D_FF ** -0.5),
    }


def reference(x_prompt, x_sample, c, cache_k, cache_v, state_ssm_fwd, state_ssm_bwd, c_ctx,
              norm1_g, norm2_g, w_ada, b_ada, w_in, q_norm_g, k_norm_g,
              lambda_q1, lambda_k1, lambda_q2, lambda_k2, attn_sub_g,
              conv_w, conv_b, A_log, dt_bias, D_skip, ssm_norm_g,
              w_branch_a, w_branch_b, w_out, w_ffn_gate, w_ffn_up, w_ffn_down):
    rope = axial_rope_tables(x_sample.shape[1])
    y_prompt = x_prompt
    y_sample = x_sample
    ks_, vs_, hfs, hbs = [], [], [], []
    for l in range(DEPTH):
        lam_init = 0.8 - 0.6 * math.exp(-0.3 * l)
        lw = dict(norm1_g=norm1_g[l], norm2_g=norm2_g[l], w_in=w_in[l],
                  q_norm_g=q_norm_g[l], k_norm_g=k_norm_g[l],
                  lambda_q1=lambda_q1[l], lambda_k1=lambda_k1[l],
                  lambda_q2=lambda_q2[l], lambda_k2=lambda_k2[l], attn_sub_g=attn_sub_g[l],
                  conv_w=conv_w[l], conv_b=conv_b[l], A_log=A_log[l], dt_bias=dt_bias[l],
                  D_skip=D_skip[l], ssm_norm_g=ssm_norm_g[l],
                  w_branch_a=w_branch_a[l], w_branch_b=w_branch_b[l], w_out=w_out[l],
                  w_ffn_gate=w_ffn_gate[l], w_ffn_up=w_ffn_up[l], w_ffn_down=w_ffn_down[l])
        mod_ctx = ada_mod(c_ctx[None, :], w_ada[l], b_ada[l])
        y_prompt, k_ctx, v_ctx, hf, hb = trunk_layer(
            y_prompt, mod_ctx, None, None, None, None, None, lam_init, **lw)
        ks_.append(k_ctx)
        vs_.append(v_ctx)
        hfs.append(hf)
        hbs.append(hb)
        mod_lat = ada_mod(c, w_ada[l], b_ada[l])
        y_sample, _, _, _, _ = trunk_layer(
            y_sample, mod_lat, rope, cache_k[:, l], cache_v[:, l],
            state_ssm_fwd[:, l], state_ssm_bwd[:, l], lam_init, **lw)
    new_cache_k = jnp.stack(ks_, axis=1)
    new_cache_v = jnp.stack(vs_, axis=1)
    new_state_ssm_fwd = jnp.stack(hfs, axis=1)
    new_state_ssm_bwd = jnp.stack(hbs, axis=1)
    return (y_prompt, y_sample, new_cache_k, new_cache_v, new_state_ssm_fwd, new_state_ssm_bwd)
```

```python
import functools
import math

import jax
import jax.numpy as jnp
from jax import lax
from jax.experimental import pallas as pl
from jax.experimental.pallas import tpu as pltpu

F32 = jnp.float32
BF16 = jnp.bfloat16

D_MODEL = 1024
GRID_W = 64
ATTN_HEADS = 8
ATTN_DH = 64
ATTN_VD = 128
HEAD_W = 2 * ATTN_DH
ROPE_BASE = 10000.0
ROPE_AXIS_DIM = ATTN_DH // 2
ROPE_HALF = ROPE_AXIS_DIM // 2
D_INNER = 2 * D_MODEL
SSM_HEADDIM = 64
SSM_HEADS = D_INNER // SSM_HEADDIM
SSM_GROUPS = 8
HEADS_PER_GROUP = SSM_HEADS // SSM_GROUPS
GROUP_W = HEADS_PER_GROUP * SSM_HEADDIM
SSM_STATE = 128
D_CONV = 5
CONV_PAD = 8
CHUNK = 128
D_FF = ((8 * D_MODEL // 3 + 255) // 256) * 256
EPS = 1e-6
LANES = 128

P_Z = 0
P_GA = P_Z + D_INNER
P_GB = P_GA + D_MODEL
P_X = P_GB + D_MODEL
P_B = P_X + D_INNER
P_C = P_B + SSM_GROUPS * SSM_STATE
P_Q = P_C + SSM_GROUPS * SSM_STATE
P_K = P_Q + D_MODEL
P_V = P_K + D_MODEL
NP = P_V + D_MODEL
P_DTYPE = F32

W_Q0 = 0
W_K0 = 1024
W_V0 = 2048
W_Z0 = 3072
W_XBC0 = W_Z0 + D_INNER
W_DT0 = W_XBC0 + D_INNER + 2 * SSM_GROUPS * SSM_STATE
W_G0 = W_DT0 + 2 * SSM_HEADS
D_IN_TOTAL = W_G0 + 2 * D_MODEL

VMEM_LIMIT = 56 * 1024 * 1024


def _cparams(sem):
    return pltpu.CompilerParams(dimension_semantics=sem, vmem_limit_bytes=VMEM_LIMIT)


def _silu(x):
    return x * jax.nn.sigmoid(x)


def _dot(a, b):
    return jnp.dot(a, b, preferred_element_type=F32)


def _dot_nt(a, b):
    return lax.dot_general(a, b, (((1,), (1,)), ((), ())), preferred_element_type=F32)


def _split3(x):
    hi = x.astype(BF16)
    r = x - hi.astype(F32)
    mid = r.astype(BF16)
    lo = (r - mid.astype(F32)).astype(BF16)
    return hi, mid, lo


def _ada_kernel(cond_ref, w_ref, b_ref, o_ref):
    s = _silu(cond_ref[...]).astype(BF16)
    o_ref[...] = _dot(s, w_ref[...].astype(BF16)) + b_ref[...]


def _ada(cond, w_ada, b_ada):
    rows = cond.shape[0]
    n = w_ada.shape[1]
    tn = 1024
    return pl.pallas_call(
        _ada_kernel,
        grid=(n // tn,),
        in_specs=[pl.BlockSpec((rows, D_MODEL), lambda j: (0, 0)),
                  pl.BlockSpec((D_MODEL, tn), lambda j: (0, j)),
                  pl.BlockSpec((1, tn), lambda j: (0, j))],
        out_specs=pl.BlockSpec((rows, tn), lambda j: (0, j)),
        out_shape=jax.ShapeDtypeStruct((rows, n), F32),
        compiler_params=_cparams(("arbitrary",)),
        name="ada",
    )(cond, w_ada, b_ada)


def _proj_kernel(x_ref, mod_ref, g_ref, w_ref, wdt_ref, p_ref, dt_ref, h_scr):
    @pl.when(pl.program_id(1) == 0)
    def _():
        x = x_ref[...]
        y = x * lax.rsqrt(jnp.mean(x * x, axis=-1, keepdims=True) + EPS) * g_ref[...]
        shift = mod_ref[0, :, 0:D_MODEL]
        scale = mod_ref[0, :, D_MODEL:2 * D_MODEL]
        h = (y * (1.0 + scale) + shift).astype(BF16)
        h_scr[...] = h
        dt_ref[...] = _dot(h, wdt_ref[...])

    p_ref[...] = _dot(h_scr[...], w_ref[...]).astype(p_ref.dtype)


def _proj(x2d, mod, g1, w_main, w_dt, tm):
    m = x2d.shape[0]
    rows_per_mod = m // mod.shape[0]
    tn = 1024
    return pl.pallas_call(
        _proj_kernel,
        grid=(m // tm, NP // tn),
        in_specs=[pl.BlockSpec((tm, D_MODEL), lambda i, j: (i, 0)),
                  pl.BlockSpec((1, 1, 6 * D_MODEL), lambda i, j: ((i * tm) // rows_per_mod, 0, 0)),
                  pl.BlockSpec((1, D_MODEL), lambda i, j: (0, 0)),
                  pl.BlockSpec((D_MODEL, tn), lambda i, j: (0, j)),
                  pl.BlockSpec((D_MODEL, LANES), lambda i, j: (0, 0))],
        out_specs=[pl.BlockSpec((tm, tn), lambda i, j: (i, j)),
                   pl.BlockSpec((tm, LANES), lambda i, j: (i, 0))],
        out_shape=[jax.ShapeDtypeStruct((m, NP), P_DTYPE),
                   jax.ShapeDtypeStruct((m, LANES), F32)],
        scratch_shapes=[pltpu.VMEM((tm, D_MODEL), BF16)],
        compiler_params=_cparams(("arbitrary", "arbitrary")),
        name="proj",
    )(x2d, mod, g1, w_main, w_dt)


def _dtprep_kernel(dtr_ref, bias_ref, alog_ref, cs_ref, rt_ref, rtT_ref):
    x = dtr_ref[...] + bias_ref[...]
    dt = jnp.maximum(x, 0.0) + jnp.log1p(jnp.exp(-jnp.abs(x)))
    a = dt * (-jnp.exp(alog_ref[...]))
    row = lax.broadcasted_iota(jnp.int32, (CHUNK, CHUNK), 0)
    col = lax.broadcasted_iota(jnp.int32, (CHUNK, CHUNK), 1)
    tril = jnp.where(col <= row, 1.0, 0.0).astype(BF16)
    triu = jnp.where(col >= row, 1.0, 0.0).astype(BF16)
    hi, mid, lo = _split3(a)
    cf = _dot(tril, hi) + _dot(tril, mid) + _dot(tril, lo)
    cb = _dot(triu, hi) + _dot(triu, mid) + _dot(triu, lo)
    cs = jnp.where(col < SSM_HEADS, cf, cb)
    rt = cs - jnp.log(dt)
    cs_ref[...] = cs
    rt_ref[...] = rt
    rtT_ref[0] = rt.T


def _dtprep(dt_raw, bias_row, alog_row):
    m = dt_raw.shape[0]
    nchunks = m // CHUNK
    return pl.pallas_call(
        _dtprep_kernel,
        grid=(nchunks,),
        in_specs=[pl.BlockSpec((CHUNK, LANES), lambda i: (i, 0)),
                  pl.BlockSpec((1, LANES), lambda i: (0, 0)),
                  pl.BlockSpec((1, LANES), lambda i: (0, 0))],
        out_specs=[pl.BlockSpec((CHUNK, LANES), lambda i: (i, 0)),
                   pl.BlockSpec((CHUNK, LANES), lambda i: (i, 0)),
                   pl.BlockSpec((1, CHUNK, CHUNK), lambda i: (i, 0, 0))],
        out_shape=[jax.ShapeDtypeStruct((m, LANES), F32),
                   jax.ShapeDtypeStruct((m, LANES), F32),
                   jax.ShapeDtypeStruct((nchunks, CHUNK, CHUNK), F32)],
        compiler_params=_cparams(("arbitrary",)),
        name="dtprep",
    )(dt_raw, bias_row, alog_row)


def _half_norm(x, g):
    lane = lax.broadcasted_iota(jnp.int32, x.shape, 1)
    lo = lane < ATTN_DH
    ss = x * x
    s_lo = jnp.sum(jnp.where(lo, ss, 0.0), axis=-1, keepdims=True)
    s_hi = jnp.sum(jnp.where(lo, 0.0, ss), axis=-1, keepdims=True)
    ms = jnp.where(lo, s_lo, s_hi) * (1.0 / ATTN_DH)
    return x * lax.rsqrt(ms + EPS) * g


def _rope(x, cos, sin_up, sin_dn):
    x_up = pltpu.roll(x, HEAD_W - ROPE_HALF, axis=1)
    x_dn = pltpu.roll(x, ROPE_HALF, axis=1)
    return x * cos + x_up * sin_up + x_dn * sin_dn


def _attn_kernel(*refs, lq, lctx, tq, lam_init, rope, emit_kv):
    it = iter(refs)
    lamp_ref, qg_ref, kg_ref, sg_ref, q_ref, k_ref, v_ref = (next(it) for _ in range(7))
    if rope:
        cos_ref, sup_ref, sdn_ref = next(it), next(it), next(it)
    if lctx:
        ck_ref, cv_ref = next(it), next(it)
    o_ref = next(it)
    if emit_kv:
        kn_ref, vo_ref = next(it), next(it)
    qq_scr, kk_scr, vv_scr = next(it), next(it), next(it)

    lp = lamp_ref[...]
    lam = (jnp.exp(jnp.sum(lp[0:1] * lp[1:2], axis=-1, keepdims=True))
           - jnp.exp(jnp.sum(lp[2:3] * lp[3:4], axis=-1, keepdims=True)) + lam_init)

    qn = _half_norm(q_ref[...].astype(F32), qg_ref[...])
    kn = _half_norm(k_ref[...].astype(F32), kg_ref[...])
    v = v_ref[...].astype(F32)
    if emit_kv:
        kn_ref[...] = kn
        vo_ref[...] = v
    if rope:
        qn = _rope(qn, cos_ref[...], sup_ref[...], sdn_ref[...])
        kn = _rope(kn, cos_ref[...], sup_ref[...], sdn_ref[...])
    qq_scr[...] = qn * (ATTN_DH ** -0.5)
    kk_scr[0:lq, :] = kn.astype(BF16)
    vv_scr[0:lq, :] = v.astype(BF16)
    if lctx:
        kk_scr[lq:lq + lctx, :] = ck_ref[0].astype(BF16)
        vv_scr[lq:lq + lctx, :] = cv_ref[0].astype(BF16)

    sub_gain = sg_ref[...] * (1.0 - lam_init)

    def block(i, carry):
        r0 = pl.multiple_of(i * tq, tq)
        qb = qq_scr[pl.ds(r0, tq), :]
        lane = lax.broadcasted_iota(jnp.int32, qb.shape, 1)
        q1 = jnp.where(lane < ATTN_DH, qb, 0.0).astype(BF16)
        q2 = jnp.where(lane < ATTN_DH, 0.0, qb).astype(BF16)
        kk = kk_scr[...]
        s1 = _dot_nt(q1, kk)
        s2 = _dot_nt(q2, kk)
        e1 = jnp.exp(s1 - jnp.max(s1, axis=-1, keepdims=True))
        e2 = jnp.exp(s2 - jnp.max(s2, axis=-1, keepdims=True))
        r1 = 1.0 / jnp.sum(e1, axis=-1, keepdims=True)
        r2 = lam / jnp.sum(e2, axis=-1, keepdims=True)
        w = (e1 * r1 - e2 * r2).astype(BF16)
        o = _dot(w, vv_scr[...])
        o = o * lax.rsqrt(jnp.mean(o * o, axis=-1, keepdims=True) + EPS) * sub_gain
        o_ref[pl.ds(r0, tq), :] = o
        return carry

    lax.fori_loop(0, lq // tq, block, 0)


def _attn(p2d, lamp, qg, kg, sg, batch, lq, lam_init, rope_tabs=None, ctx=None, emit_kv=False):
    m = p2d.shape[0]
    lctx = 0 if ctx is None else ctx[0].shape[1]
    tq = min(lq, 256)
    qb, kb, vb = P_Q // HEAD_W, P_K // HEAD_W, P_V // HEAD_W
    const = lambda b, h: (0, 0)
    in_specs = [pl.BlockSpec((4, ATTN_DH), const),
                pl.BlockSpec((1, HEAD_W), const),
                pl.BlockSpec((1, HEAD_W), const),
                pl.BlockSpec((1, ATTN_VD), const),
                pl.BlockSpec((lq, HEAD_W), lambda b, h: (b, qb + h)),
                pl.BlockSpec((lq, HEAD_W), lambda b, h: (b, kb + h)),
                pl.BlockSpec((lq, HEAD_W), lambda b, h: (b, vb + h))]
    args = [lamp, qg, kg, sg, p2d, p2d, p2d]
    if rope_tabs is not None:
        in_specs += [pl.BlockSpec((lq, HEAD_W), const)] * 3
        args += list(rope_tabs)
    if ctx is not None:
        in_specs += [pl.BlockSpec((1, lctx, HEAD_W), lambda b, h: (b, 0, h))] * 2
        args += list(ctx)
    head_out = pl.BlockSpec((lq, HEAD_W), lambda b, h: (b, h))
    out_specs = [head_out]
    out_shape = [jax.ShapeDtypeStruct((m, D_MODEL), F32)]
    if emit_kv:
        out_specs += [head_out, head_out]
        out_shape += [jax.ShapeDtypeStruct((m, D_MODEL), F32)] * 2
    kern = functools.partial(_attn_kernel, lq=lq, lctx=lctx, tq=tq, lam_init=lam_init,
                             rope=rope_tabs is not None, emit_kv=emit_kv)
    return pl.pallas_call(
        kern,
        grid=(batch, ATTN_HEADS),
        in_specs=in_specs,
        out_specs=out_specs,
        out_shape=out_shape,
        scratch_shapes=[pltpu.VMEM((lq, HEAD_W), F32),
                        pltpu.VMEM((lq + lctx, HEAD_W), BF16),
                        pltpu.VMEM((lq + lctx, ATTN_VD), BF16)],
        compiler_params=_cparams(("arbitrary", "arbitrary")),
        name="attn",
    )(*args)


def _expand_heads(cols, blk):
    out = cols[-1]
    for j in range(HEADS_PER_GROUP - 2, -1, -1):
        out = jnp.where(blk == j, cols[j], out)
    return out


def _ssd_kernel(*refs, seq, has_h0):
    it = iter(refs)
    x_ref, b_ref, c_ref = next(it), next(it), next(it)
    cwx_ref, cwb_ref, cwc_ref = next(it), next(it), next(it)
    cbx_ref, cbb_ref, cbc_ref = next(it), next(it), next(it)
    dexp_ref, cs_ref, rt_ref, rtT_ref = next(it), next(it), next(it), next(it)
    if has_h0:
        h0f_ref, h0b_ref = next(it), next(it)
    y_ref, hf_ref, hb_ref = next(it), next(it), next(it)
    xpad, bpad, cpad, xc_s, bc_s, cc_s, stf, stb, yacc = (next(it) for _ in range(9))

    g = pl.program_id(1)
    nc = seq // CHUNK

    def conv(in_ref, w_ref, bias_ref, pad, out):
        width = pad.shape[1]
        zeros = jnp.zeros((CONV_PAD, width), F32)
        pad[0:CONV_PAD, :] = zeros
        pad[CONV_PAD + seq:2 * CONV_PAD + seq, :] = zeros
        pad[CONV_PAD:CONV_PAD + seq, :] = in_ref[...].astype(F32)

        def body(c, carry):
            r0 = pl.multiple_of(c * CHUNK, CHUNK)
            acc = jnp.broadcast_to(bias_ref[...], (CHUNK, width))
            win = pad[pl.ds(r0, CHUNK + 2 * CONV_PAD), :]
            for j in range(D_CONV):
                off = CONV_PAD - D_CONV // 2 + j
                acc = acc + win[off:off + CHUNK, :] * w_ref[j:j + 1, :]
            out[pl.ds(r0, CHUNK), :] = _silu(acc).astype(out.dtype)
            return carry

        lax.fori_loop(0, nc, body, 0)

    conv(x_ref, cwx_ref, cbx_ref, xpad, xc_s)
    conv(b_ref, cwb_ref, cbb_ref, bpad, bc_s)
    conv(c_ref, cwc_ref, cbc_ref, cpad, cc_s)

    if has_h0:
        stf[...] = h0f_ref[0].T
        stb[...] = h0b_ref[0].T
    else:
        stf[...] = jnp.zeros_like(stf)
        stb[...] = jnp.zeros_like(stb)

    row = lax.broadcasted_iota(jnp.int32, (CHUNK, CHUNK), 0)
    col = lax.broadcasted_iota(jnp.int32, (CHUNK, CHUNK), 1)
    causal = col <= row
    anti = col >= row
    blk = lax.broadcasted_iota(jnp.int32, (CHUNK, GROUP_W), 1) // SSM_HEADDIM
    shift = (LANES - HEADS_PER_GROUP * g) % LANES
    neg_inf = -jnp.inf

    def chunk_inputs(c):
        r0 = pl.multiple_of(c * CHUNK, CHUNK)
        sl = pl.ds(r0, CHUNK)
        cs = pltpu.roll(cs_ref[sl, :], shift, axis=1)
        rt = pltpu.roll(rt_ref[sl, :], shift, axis=1)
        return sl, cs, rt

    def fwd_body(c, carry):
        sl, cs, rt = chunk_inputs(c)
        xc = xc_s[sl, :]
        bc = bc_s[sl, :]
        cc = cc_s[sl, :]
        gmat = _dot_nt(cc, bc)
        y = dexp_ref[...] * xc
        for j in range(HEADS_PER_GROUP):
            rf = rtT_ref[c, pl.ds(HEADS_PER_GROUP * g + j, 1), :]
            rb = rtT_ref[c, pl.ds(SSM_HEADS + HEADS_PER_GROUP * g + j, 1), :]
            lf = jnp.exp(jnp.where(causal, cs[:, j:j + 1] - rf, neg_inf))
            lb = jnp.exp(jnp.where(anti, cs[:, SSM_HEADS + j:SSM_HEADS + j + 1] - rb, neg_inf))
            mj = (gmat * (lf + lb)).astype(BF16)
            xj = jnp.where(blk == j, xc, 0.0).astype(BF16)
            y = y + _dot(mj, xj)
        cs_e = _expand_heads([cs[:, j:j + 1] for j in range(HEADS_PER_GROUP)], blk)
        rt_e = _expand_heads([rt[:, j:j + 1] for j in range(HEADS_PER_GROUP)], blk)
        st = stf[...]
        y = y + jnp.exp(cs_e) * _dot(cc, st.astype(BF16))
        last = cs_e[CHUNK - 1:CHUNK, :]
        xw = (xc * jnp.exp(last - rt_e)).astype(BF16)
        stf[...] = jnp.exp(last) * st + _dot(bc.T, xw)
        yacc[sl, :] = y
        return carry

    def bwd_body(i, carry):
        c = nc - 1 - i
        sl, cs, rt = chunk_inputs(c)
        xc = xc_s[sl, :]
        bc = bc_s[sl, :]
        cc = cc_s[sl, :]
        cs_e = _expand_heads([cs[:, SSM_HEADS + j:SSM_HEADS + j + 1] for j in range(HEADS_PER_GROUP)], blk)
        rt_e = _expand_heads([rt[:, SSM_HEADS + j:SSM_HEADS + j + 1] for j in range(HEADS_PER_GROUP)], blk)
        st = stb[...]
        yacc[sl, :] = yacc[sl, :] + jnp.exp(cs_e) * _dot(cc, st.astype(BF16))
        first = cs_e[0:1, :]
        xw = (xc * jnp.exp(first - rt_e)).astype(BF16)
        stb[...] = jnp.exp(first) * st + _dot(bc.T, xw)
        return carry

    lax.fori_loop(0, nc, fwd_body, 0)
    lax.fori_loop(0, nc, bwd_body, 0)

    y_ref[...] = yacc[...].astype(y_ref.dtype)
    hf_ref[0] = stf[...].T
    hb_ref[0] = stb[...].T


def _ssd(p2d, cs, rt, rtT, conv_w, conv_b, dexp, batch, seq, h0=None):
    m = p2d.shape[0]
    nc = seq // CHUNK
    xb, bb, cb = P_X // GROUP_W, P_B // SSM_STATE, P_C // SSM_STATE
    cw_b0 = D_INNER // SSM_STATE
    cw_c0 = cw_b0 + SSM_GROUPS
    in_specs = [pl.BlockSpec((seq, GROUP_W), lambda b, g: (b, xb + g)),
                pl.BlockSpec((seq, SSM_STATE), lambda b, g: (b, bb + g)),
                pl.BlockSpec((seq, SSM_STATE), lambda b, g: (b, cb + g)),
                pl.BlockSpec((D_CONV, GROUP_W), lambda b, g: (0, g)),
                pl.BlockSpec((D_CONV, SSM_STATE), lambda b, g: (0, cw_b0 + g)),
                pl.BlockSpec((D_CONV, SSM_STATE), lambda b, g: (0, cw_c0 + g)),
                pl.BlockSpec((1, GROUP_W), lambda b, g: (0, g)),
                pl.BlockSpec((1, SSM_STATE), lambda b, g: (0, cw_b0 + g)),
                pl.BlockSpec((1, SSM_STATE), lambda b, g: (0, cw_c0 + g)),
                pl.BlockSpec((1, GROUP_W), lambda b, g: (0, g)),
                pl.BlockSpec((seq, LANES), lambda b, g: (b, 0)),
                pl.BlockSpec((seq, LANES), lambda b, g: (b, 0)),
                pl.BlockSpec((nc, CHUNK, CHUNK), lambda b, g: (b, 0, 0))]
    args = [p2d, p2d, p2d, conv_w, conv_w, conv_w, conv_b, conv_b, conv_b, dexp, cs, rt, rtT]
    state_spec = pl.BlockSpec((1, GROUP_W, SSM_STATE), lambda b, g: (b, g, 0))
    if h0 is not None:
        in_specs += [state_spec, state_spec]
        args += list(h0)
    state_shape = jax.ShapeDtypeStruct((batch, D_INNER, SSM_STATE), F32)
    kern = functools.partial(_ssd_kernel, seq=seq, has_h0=h0 is not None)
    return pl.pallas_call(
        kern,
        grid=(batch, SSM_GROUPS),
        in_specs=in_specs,
        out_specs=[pl.BlockSpec((seq, GROUP_W), lambda b, g: (b, g)), state_spec, state_spec],
        out_shape=[jax.ShapeDtypeStruct((m, D_INNER), F32), state_shape, state_shape],
        scratch_shapes=[pltpu.VMEM((seq + 2 * CONV_PAD, GROUP_W), F32),
                        pltpu.VMEM((seq + 2 * CONV_PAD, SSM_STATE), F32),
                        pltpu.VMEM((seq + 2 * CONV_PAD, SSM_STATE), F32),
                        pltpu.VMEM((seq, GROUP_W), F32),
                        pltpu.VMEM((seq, SSM_STATE), BF16),
                        pltpu.VMEM((seq, SSM_STATE), BF16),
                        pltpu.VMEM((SSM_STATE, GROUP_W), F32),
                        pltpu.VMEM((SSM_STATE, GROUP_W), F32),
                        pltpu.VMEM((seq, GROUP_W), F32)],
        compiler_params=_cparams(("arbitrary", "arbitrary")),
        name="ssd",
    )(*args)


def _merge_kernel(x_ref, mod_ref, o_ref, y_ref, z_ref, ga_ref, gb_ref, sg_ref,
                  wa_ref, wb_ref, wo_ref, out_ref):
    out_a = _dot(o_ref[...].astype(BF16), wa_ref[...])
    yz = y_ref[...].astype(F32) * _silu(z_ref[...].astype(F32))
    yn = yz * lax.rsqrt(jnp.mean(yz * yz, axis=-1, keepdims=True) + EPS) * sg_ref[...]
    out_b = _dot(yn.astype(BF16), wb_ref[...])
    merged = (jax.nn.sigmoid(ga_ref[...].astype(F32)) * out_a
              + jax.nn.sigmoid(gb_ref[...].astype(F32)) * out_b)
    gate1 = mod_ref[0, :, 2 * D_MODEL:3 * D_MODEL]
    out_ref[...] = x_ref[...] + gate1 * _dot(merged.astype(BF16), wo_ref[...])


def _resident(shape):
    return pl.BlockSpec(shape, lambda i: (0,) * len(shape), pipeline_mode=pl.Buffered(1))


def _merge(x2d, mod, o, y, p2d, ssm_g, wa, wb, wo, tm):
    m = x2d.shape[0]
    rows_per_mod = m // mod.shape[0]
    return pl.pallas_call(
        _merge_kernel,
        grid=(m // tm,),
        in_specs=[pl.BlockSpec((tm, D_MODEL), lambda i: (i, 0)),
                  pl.BlockSpec((1, 1, 6 * D_MODEL), lambda i: ((i * tm) // rows_per_mod, 0, 0)),
                  pl.BlockSpec((tm, D_MODEL), lambda i: (i, 0)),
                  pl.BlockSpec((tm, D_INNER), lambda i: (i, 0)),
                  pl.BlockSpec((tm, D_INNER), lambda i: (i, P_Z // D_INNER)),
                  pl.BlockSpec((tm, D_MODEL), lambda i: (i, P_GA // D_MODEL)),
                  pl.BlockSpec((tm, D_MODEL), lambda i: (i, P_GB // D_MODEL)),
                  _resident((1, D_INNER)),
                  _resident((D_MODEL, D_MODEL)),
                  _resident((D_INNER, D_MODEL)),
                  _resident((D_MODEL, D_MODEL))],
        out_specs=pl.BlockSpec((tm, D_MODEL), lambda i: (i, 0)),
        out_shape=jax.ShapeDtypeStruct((m, D_MODEL), F32),
        compiler_params=_cparams(("arbitrary",)),
        name="merge",
    )(x2d, mod, o, y, p2d, p2d, p2d, ssm_g, wa, wb, wo)


def _ffn_kernel(x_ref, mod_ref, g_ref, wg_ref, wu_ref, wd_ref, out_ref):
    x = x_ref[...]
    y = x * lax.rsqrt(jnp.mean(x * x, axis=-1, keepdims=True) + EPS) * g_ref[...]
    shift = mod_ref[0, :, 3 * D_MODEL:4 * D_MODEL]
    scale = mod_ref[0, :, 4 * D_MODEL:5 * D_MODEL]
    gate = mod_ref[0, :, 5 * D_MODEL:6 * D_MODEL]
    h = (y * (1.0 + scale) + shift).astype(BF16)
    f = (_silu(_dot(h, wg_ref[...])) * _dot(h, wu_ref[...])).astype(BF16)
    out_ref[...] = x + gate * _dot(f, wd_ref[...])


def _ffn(x2d, mod, g2, wg, wu, wd, tm):
    m = x2d.shape[0]
    rows_per_mod = m // mod.shape[0]
    return pl.pallas_call(
        _ffn_kernel,
        grid=(m // tm,),
        in_specs=[pl.BlockSpec((tm, D_MODEL), lambda i: (i, 0)),
                  pl.BlockSpec((1, 1, 6 * D_MODEL), lambda i: ((i * tm) // rows_per_mod, 0, 0)),
                  _resident((1, D_MODEL)),
                  _resident((D_MODEL, D_FF)),
                  _resident((D_MODEL, D_FF)),
                  _resident((D_FF, D_MODEL))],
        out_specs=pl.BlockSpec((tm, D_MODEL), lambda i: (i, 0)),
        out_shape=jax.ShapeDtypeStruct((m, D_MODEL), F32),
        compiler_params=_cparams(("arbitrary",)),
        name="ffn",
    )(x2d, mod, g2, wg, wu, wd)


def _rope_tables(n_tok):
    pos = jnp.arange(n_tok, dtype=jnp.int32)
    row = (pos // GRID_W).astype(F32)
    colp = (pos % GRID_W).astype(F32)
    inv = ROPE_BASE ** (-jnp.arange(0, ROPE_AXIS_DIM, 2, dtype=F32) / ROPE_AXIS_DIM)
    ang_r = row[:, None] * inv[None, :]
    ang_c = colp[:, None] * inv[None, :]
    zero = jnp.zeros_like(ang_r)
    cos = jnp.concatenate([jnp.cos(ang_r)] * 2 + [jnp.cos(ang_c)] * 2, axis=-1)
    sup = jnp.concatenate([-jnp.sin(ang_r), zero, -jnp.sin(ang_c), zero], axis=-1)
    sdn = jnp.concatenate([zero, jnp.sin(ang_r), zero, jnp.sin(ang_c)], axis=-1)
    return tuple(jnp.tile(t, (1, 2)) for t in (cos, sup, sdn))


def _pad_lanes(v):
    flat = v.reshape(1, -1).astype(F32)
    return jnp.pad(flat, ((0, 0), (0, LANES - flat.shape[1])))


def kernel(x_prompt, x_sample, c, cache_k, cache_v, state_ssm_fwd, state_ssm_bwd, c_ctx, norm1_g, norm2_g, w_ada, b_ada, w_in, q_norm_g, k_norm_g, lambda_q1, lambda_k1, lambda_q2, lambda_k2, attn_sub_g, conv_w, conv_b, A_log, dt_bias, D_skip, ssm_norm_g, w_branch_a, w_branch_b, w_out, w_ffn_gate, w_ffn_up, w_ffn_down):
    depth = norm1_g.shape[0]
    assert depth == 1, "single-layer kernel"
    l = 0
    lam_init = 0.8 - 0.6 * math.exp(-0.3 * l)
    nb, seq, _ = x_prompt.shape
    db, dseq, _ = x_sample.shape

    cond = jnp.concatenate([c_ctx[None, :], c, jnp.zeros((8 - 1 - db, D_MODEL), F32)], axis=0)
    mod = _ada(cond, w_ada[l], b_ada[l][None, :])
    mod_ctx = mod[0:1].reshape(1, 1, 6 * D_MODEL)
    mod_lat = mod[1:1 + db].reshape(db, 1, 6 * D_MODEL)

    w = w_in[l]
    w_main = jnp.concatenate([w[:, W_Z0:W_Z0 + D_INNER], w[:, W_G0:W_G0 + 2 * D_MODEL],
                              w[:, W_XBC0:W_DT0], w[:, W_Q0:W_Z0]], axis=1).astype(BF16)
    w_dt = jnp.pad(w[:, W_DT0:W_G0], ((0, 0), (0, LANES - 2 * SSM_HEADS))).astype(BF16)
    wa = w_branch_a[l].astype(BF16)
    wb = w_branch_b[l].astype(BF16)
    wo = w_out[l].astype(BF16)
    wg = w_ffn_gate[l].astype(BF16)
    wu = w_ffn_up[l].astype(BF16)
    wd = w_ffn_down[l].astype(BF16)
    g1 = norm1_g[l][None, :]
    g2 = norm2_g[l][None, :]
    qg = jnp.tile(q_norm_g[l], 2)[None, :]
    kg = jnp.tile(k_norm_g[l], 2)[None, :]
    sg = attn_sub_g[l][None, :]
    lamp = jnp.stack([lambda_q1[l], lambda_k1[l], lambda_q2[l], lambda_k2[l]], axis=0)
    bias_row = _pad_lanes(dt_bias[l])
    alog_row = _pad_lanes(A_log[l])
    dexp = jnp.repeat(D_skip[l], SSM_HEADDIM)[None, :]
    cw = conv_w[l]
    cb = conv_b[l][None, :]
    ssm_g = ssm_norm_g[l][None, :]

    def layer(x, mod_rows, rope_tabs, ctx, h0, emit_kv):
        batch, sl, _ = x.shape
        x2d = x.reshape(batch * sl, D_MODEL)
        p2d, dt_raw = _proj(x2d, mod_rows, g1, w_main, w_dt, tm=1024)
        cs, rt, rtT = _dtprep(dt_raw, bias_row, alog_row)
        attn_out = _attn(p2d, lamp, qg, kg, sg, batch, sl, lam_init, rope_tabs, ctx, emit_kv)
        y, hf, hb = _ssd(p2d, cs, rt, rtT, cw, cb, dexp, batch, sl, h0)
        x1 = _merge(x2d, mod_rows, attn_out[0], y, p2d, ssm_g, wa, wb, wo, tm=512)
        x2 = _ffn(x1, mod_rows, g2, wg, wu, wd, tm=512)
        return x2.reshape(batch, sl, D_MODEL), attn_out[1:], hf, hb

    y_prompt, kv, hf, hb = layer(x_prompt, mod_ctx, None, None, None, True)
    ctx = (cache_k[:, l].reshape(db, -1, D_MODEL), cache_v[:, l].reshape(db, -1, D_MODEL))
    h0 = (state_ssm_fwd[:, l].reshape(db, D_INNER, SSM_STATE),
          state_ssm_bwd[:, l].reshape(db, D_INNER, SSM_STATE))
    y_sample, _, _, _ = layer(x_sample, mod_lat, _rope_tables(dseq), ctx, h0, False)

    new_cache_k = kv[0].reshape(nb, 1, seq, ATTN_HEADS, 2, ATTN_DH)
    new_cache_v = kv[1].reshape(nb, 1, seq, ATTN_HEADS, ATTN_VD)
    new_hf = hf.reshape(nb, 1, SSM_HEADS, SSM_HEADDIM, SSM_STATE)
    new_hb = hb.reshape(nb, 1, SSM_HEADS, SSM_HEADDIM, SSM_STATE)
    return (y_prompt, y_sample, new_cache_k, new_cache_v, new_hf, new_hb)
```

```python
import functools
import math

import jax
import jax.numpy as jnp
from jax import lax
from jax.experimental import pallas as pl
from jax.experimental.pallas import tpu as pltpu

F32 = jnp.float32
BF16 = jnp.bfloat16

D_MODEL = 1024
GRID_W = 64
ATTN_HEADS = 8
ATTN_DH = 64
ATTN_VD = 128
HEAD_W = 2 * ATTN_DH
ROPE_BASE = 10000.0
ROPE_AXIS_DIM = ATTN_DH // 2
ROPE_HALF = ROPE_AXIS_DIM // 2
D_INNER = 2 * D_MODEL
SSM_HEADDIM = 64
SSM_HEADS = D_INNER // SSM_HEADDIM
SSM_GROUPS = 8
HEADS_PER_GROUP = SSM_HEADS // SSM_GROUPS
GROUP_W = HEADS_PER_GROUP * SSM_HEADDIM
SSM_STATE = 128
D_CONV = 5
CONV_PAD = 8
CONV_EDGE = 16
CHUNK = 128
D_FF = ((8 * D_MODEL // 3 + 255) // 256) * 256
EPS = 1e-6
LANES = 128
SUBLANES = 8
LOG2E = math.log2(math.e)

P_Z = 0
P_GA = P_Z + D_INNER
P_GB = P_GA + D_MODEL
P_X = P_GB + D_MODEL
P_B = P_X + D_INNER
P_C = P_B + SSM_GROUPS * SSM_STATE
P_Q = P_C + SSM_GROUPS * SSM_STATE
P_K = P_Q + D_MODEL
P_V = P_K + D_MODEL
NP = P_V + D_MODEL
PROJ_TN = 1024
ACT_DTYPE = BF16

W_Q0 = 0
W_Z0 = 3 * D_MODEL
W_XBC0 = W_Z0 + D_INNER
W_DT0 = W_XBC0 + D_INNER + 2 * SSM_GROUPS * SSM_STATE
W_G0 = W_DT0 + 2 * SSM_HEADS

VMEM_LIMIT = 56 * 1024 * 1024


def _cparams(sem):
    return pltpu.CompilerParams(dimension_semantics=sem, vmem_limit_bytes=VMEM_LIMIT)


def _silu(x):
    return x * jax.nn.sigmoid(x)


def _dot(a, b):
    return jnp.dot(a, b, preferred_element_type=F32)


def _dot_nt(a, b):
    return lax.dot_general(a, b, (((1,), (1,)), ((), ())), preferred_element_type=F32)


def _split3(x):
    hi = x.astype(BF16)
    r = x - hi.astype(F32)
    mid = r.astype(BF16)
    lo = (r - mid.astype(F32)).astype(BF16)
    return hi, mid, lo


def _resident(shape):
    return pl.BlockSpec(shape, lambda *_: (0,) * len(shape), pipeline_mode=pl.Buffered(1))


def _ada_kernel(cond_ref, w_ref, b_ref, o_ref):
    s = _silu(cond_ref[...]).astype(BF16)
    o_ref[...] = _dot(s, w_ref[...].astype(BF16)) + b_ref[...]


def _ada(cond, w_ada, b_ada):
    rows = cond.shape[0]
    n = w_ada.shape[1]
    tn = 1024
    return pl.pallas_call(
        _ada_kernel,
        grid=(n // tn,),
        in_specs=[pl.BlockSpec((rows, D_MODEL), lambda j: (0, 0)),
                  pl.BlockSpec((D_MODEL, tn), lambda j: (0, j)),
                  pl.BlockSpec((1, tn), lambda j: (0, j))],
        out_specs=pl.BlockSpec((rows, tn), lambda j: (0, j)),
        out_shape=jax.ShapeDtypeStruct((rows, n), F32),
        compiler_params=_cparams(("arbitrary",)),
        name="ada",
    )(cond, w_ada, b_ada)


def _norm1_kernel(x_ref, mod_ref, g_ref, wdt_ref, h_ref, dt_ref):
    x = x_ref[...]
    y = x * lax.rsqrt(jnp.mean(x * x, axis=-1, keepdims=True) + EPS) * g_ref[...]
    shift = mod_ref[0, :, 0:D_MODEL]
    scale = mod_ref[0, :, D_MODEL:2 * D_MODEL]
    h = (y * (1.0 + scale) + shift).astype(BF16)
    h_ref[...] = h
    dt_ref[...] = _dot_nt(h, wdt_ref[...].astype(BF16))


def _norm1(x2d, mod, g1, w_dt, tm):
    m = x2d.shape[0]
    rows_per_mod = m // mod.shape[0]
    return pl.pallas_call(
        _norm1_kernel,
        grid=(m // tm,),
        in_specs=[pl.BlockSpec((tm, D_MODEL), lambda i: (i, 0)),
                  pl.BlockSpec((1, 1, 6 * D_MODEL), lambda i: ((i * tm) // rows_per_mod, 0, 0)),
                  _resident((1, D_MODEL)),
                  _resident((LANES, D_MODEL))],
        out_specs=[pl.BlockSpec((tm, D_MODEL), lambda i: (i, 0)),
                   pl.BlockSpec((tm, LANES), lambda i: (i, 0))],
        out_shape=[jax.ShapeDtypeStruct((m, D_MODEL), BF16),
                   jax.ShapeDtypeStruct((m, LANES), F32)],
        compiler_params=_cparams(("arbitrary",)),
        name="norm1",
    )(x2d, mod, g1, w_dt)


_GATE_TILE0 = P_GA // PROJ_TN
_N_GATE_TILES = 2 * D_MODEL // PROJ_TN


def _w_in_tile(j):
    z_tiles = D_INNER // PROJ_TN
    xbc_tile0 = (P_X // PROJ_TN)
    qkv_tile0 = (P_Q // PROJ_TN)
    z_src = jnp.minimum(j, z_tiles - 1) + W_Z0 // PROJ_TN
    xbc_src = j - xbc_tile0 + W_XBC0 // PROJ_TN
    qkv_src = j - qkv_tile0 + W_Q0 // PROJ_TN
    return jnp.where(j < xbc_tile0, z_src, jnp.where(j < qkv_tile0, xbc_src, qkv_src))


def _proj_kernel(h_ref, win_ref, wg_ref, p_ref, w_scr, *, tm):
    j = pl.program_id(0)
    i = pl.program_id(1)
    is_gate = jnp.logical_and(j >= _GATE_TILE0, j < _GATE_TILE0 + _N_GATE_TILES)

    @pl.when(jnp.logical_and(i == 0, is_gate))
    def _():
        w_scr[...] = wg_ref[...].astype(BF16)

    @pl.when(jnp.logical_and(i == 0, jnp.logical_not(is_gate)))
    def _():
        w_scr[...] = win_ref[...].astype(BF16)

    r0 = pl.multiple_of(i * tm, tm)
    p_ref[...] = _dot_nt(h_ref[pl.ds(r0, tm), :], w_scr[...]).astype(p_ref.dtype)


def _proj(h, w_in_t, w_gates_t, tm):
    m = h.shape[0]
    tn = PROJ_TN
    return pl.pallas_call(
        functools.partial(_proj_kernel, tm=tm),
        grid=(NP // tn, m // tm),
        in_specs=[_resident((m, D_MODEL)),
                  pl.BlockSpec((tn, D_MODEL), lambda j, i: (_w_in_tile(j), 0)),
                  pl.BlockSpec((tn, D_MODEL),
                               lambda j, i: (jnp.clip(j - _GATE_TILE0, 0, _N_GATE_TILES - 1), 0))],
        out_specs=pl.BlockSpec((tm, tn), lambda j, i: (i, j)),
        out_shape=jax.ShapeDtypeStruct((m, NP), ACT_DTYPE),
        scratch_shapes=[pltpu.VMEM((tn, D_MODEL), BF16)],
        compiler_params=_cparams(("arbitrary", "arbitrary")),
        name="proj",
    )(h, w_in_t, w_gates_t)


DTPREP_CHUNKS = 4


def _dtprep_kernel(dtr_ref, bias_ref, alog_ref, cs_ref, rtT_ref):
    row = lax.broadcasted_iota(jnp.int32, (CHUNK, CHUNK), 0)
    col = lax.broadcasted_iota(jnp.int32, (CHUNK, CHUNK), 1)
    tril = jnp.where(col <= row, 1.0, 0.0).astype(BF16)
    triu = jnp.where(col >= row, 1.0, 0.0).astype(BF16)
    neg_a = -jnp.exp(alog_ref[...])
    for c in range(DTPREP_CHUNKS):
        sl = slice(c * CHUNK, (c + 1) * CHUNK)
        x = dtr_ref[sl, :] + bias_ref[...]
        dt = jnp.maximum(x, 0.0) + jnp.log1p(jnp.exp(-jnp.abs(x)))
        hi, mid, lo = _split3(dt * neg_a)
        cf = _dot(tril, hi) + _dot(tril, mid) + _dot(tril, lo)
        cb = _dot(triu, hi) + _dot(triu, mid) + _dot(triu, lo)
        cs = jnp.where(col < SSM_HEADS, cf, cb)
        cs_ref[sl, :] = cs
        rtT_ref[c] = (cs - jnp.log(dt)).T


def _dtprep(dt_raw, bias_row, alog_row):
    m = dt_raw.shape[0]
    rows = DTPREP_CHUNKS * CHUNK
    return pl.pallas_call(
        _dtprep_kernel,
        grid=(m // rows,),
        in_specs=[pl.BlockSpec((rows, LANES), lambda i: (i, 0)),
                  _resident((1, LANES)),
                  _resident((1, LANES))],
        out_specs=[pl.BlockSpec((rows, LANES), lambda i: (i, 0)),
                   pl.BlockSpec((DTPREP_CHUNKS, CHUNK, CHUNK), lambda i: (i, 0, 0))],
        out_shape=[jax.ShapeDtypeStruct((m, LANES), F32),
                   jax.ShapeDtypeStruct((m // CHUNK, CHUNK, CHUNK), F32)],
        compiler_params=_cparams(("arbitrary",)),
        name="dtprep",
    )(dt_raw, bias_row, alog_row)


def _half_norm(x, g, half_ones):
    ss = x * x
    hi = ss.astype(BF16)
    lo = (ss - hi.astype(F32)).astype(BF16)
    ms = (_dot(hi, half_ones) + _dot(lo, half_ones)) * (1.0 / ATTN_DH)
    return x * lax.rsqrt(ms + EPS) * g


def _rope(x, cos, sin_up, sin_dn):
    x_up = pltpu.roll(x, HEAD_W - ROPE_HALF, axis=1)
    x_dn = pltpu.roll(x, ROPE_HALF, axis=1)
    return x * cos + x_up * sin_up + x_dn * sin_dn


def _attn_kernel(*refs, lq, lctx, tq, lam_init, rope, emit_kv):
    it = iter(refs)
    lamp_ref, qg_ref, kg_ref, sg_ref, q_ref, k_ref, v_ref = (next(it) for _ in range(7))
    if rope:
        cos_ref, sup_ref, sdn_ref = next(it), next(it), next(it)
    if lctx:
        ckT_ref, cv_ref = next(it), next(it)
    o_ref = next(it)
    if emit_kv:
        knT_ref, vo_ref = next(it), next(it)
    q1_scr, q2_scr, kk_scr, va_scr = next(it), next(it), next(it), next(it)
    if lctx:
        kcT_scr = next(it)

    lp = lamp_ref[...]
    lam = (jnp.exp(jnp.sum(lp[0:1] * lp[1:2], axis=-1, keepdims=True))
           - jnp.exp(jnp.sum(lp[2:3] * lp[3:4], axis=-1, keepdims=True)) + lam_init)

    r_i = lax.broadcasted_iota(jnp.int32, (HEAD_W, HEAD_W), 0) // ATTN_DH
    c_i = lax.broadcasted_iota(jnp.int32, (HEAD_W, HEAD_W), 1) // ATTN_DH
    half_ones = jnp.where(r_i == c_i, 1.0, 0.0).astype(BF16)

    qn = _half_norm(q_ref[...].astype(F32), qg_ref[...], half_ones)
    kn = _half_norm(k_ref[...].astype(F32), kg_ref[...], half_ones)
    v = v_ref[...]
    if emit_kv:
        knT_ref[0, 0] = kn.T
        vo_ref[...] = v.astype(F32)
    if rope:
        qn = _rope(qn, cos_ref[...], sup_ref[...], sdn_ref[...])
        kn = _rope(kn, cos_ref[...], sup_ref[...], sdn_ref[...])
    qs = qn * (ATTN_DH ** -0.5 * LOG2E)
    lane = lax.broadcasted_iota(jnp.int32, qs.shape, 1)
    q1_scr[...] = jnp.where(lane < ATTN_DH, qs, 0.0).astype(BF16)
    q2_scr[...] = jnp.where(lane < ATTN_DH, 0.0, qs).astype(BF16)
    kk_scr[...] = kn.astype(BF16)
    va_scr[0:lq, 0:ATTN_VD] = v.astype(BF16)
    va_scr[:, ATTN_VD:2 * ATTN_VD] = jnp.ones((lq + lctx, ATTN_VD), BF16)
    if lctx:
        kcT_scr[...] = ckT_ref[0, 0].astype(BF16)
        head = pl.program_id(1)
        for hh in range(ATTN_HEADS):
            @pl.when(head == hh)
            def _(hh=hh):
                rows = cv_ref[0, pl.ds(hh, lctx, stride=ATTN_HEADS), :]
                va_scr[lq:lq + lctx, 0:ATTN_VD] = rows.astype(BF16)

    sub_gain = sg_ref[...] * (1.0 - lam_init)

    def scores(qb):
        s = _dot_nt(qb, kk_scr[...])
        if lctx:
            s = jnp.concatenate([s, _dot(qb, kcT_scr[...])], axis=1)
        return s

    def softmax_pv(qb):
        s = scores(qb)
        p = jnp.exp2(s - jnp.max(s, axis=-1, keepdims=True)).astype(BF16)
        a = _dot(p, va_scr[...])
        return a[:, 0:ATTN_VD] / a[:, ATTN_VD:2 * ATTN_VD]

    def block(i, carry):
        r0 = pl.multiple_of(i * tq, tq)
        o = softmax_pv(q1_scr[pl.ds(r0, tq), :]) - lam * softmax_pv(q2_scr[pl.ds(r0, tq), :])
        o = o * lax.rsqrt(jnp.mean(o * o, axis=-1, keepdims=True) + EPS) * sub_gain
        o_ref[pl.ds(r0, tq), :] = o.astype(o_ref.dtype)
        return carry

    lax.fori_loop(0, lq // tq, block, 0)


def _attn(p2d, lamp, qg, kg, sg, batch, lq, lam_init, rope_tabs=None, ctx=None, emit_kv=False):
    m = p2d.shape[0]
    lctx = 0 if ctx is None else ctx[0].shape[-1]
    tq = min(lq, 1024)
    qb, kb, vb = P_Q // HEAD_W, P_K // HEAD_W, P_V // HEAD_W
    const = lambda b, h: (0, 0)
    in_specs = [pl.BlockSpec((4, ATTN_DH), const),
                pl.BlockSpec((1, HEAD_W), const),
                pl.BlockSpec((1, HEAD_W), const),
                pl.BlockSpec((1, ATTN_VD), const),
                pl.BlockSpec((lq, HEAD_W), lambda b, h: (b, qb + h)),
                pl.BlockSpec((lq, HEAD_W), lambda b, h: (b, kb + h)),
                pl.BlockSpec((lq, HEAD_W), lambda b, h: (b, vb + h))]
    args = [lamp, qg, kg, sg, p2d, p2d, p2d]
    if rope_tabs is not None:
        in_specs += [pl.BlockSpec((lq, HEAD_W), const)] * 3
        args += list(rope_tabs)
    if ctx is not None:
        in_specs += [pl.BlockSpec((1, 1, HEAD_W, lctx), lambda b, h: (b, h, 0, 0)),
                     pl.BlockSpec((1, lctx * ATTN_HEADS, ATTN_VD), lambda b, h: (b, 0, 0))]
        args += list(ctx)
    head_out = pl.BlockSpec((lq, HEAD_W), lambda b, h: (b, h))
    out_specs = [head_out]
    out_shape = [jax.ShapeDtypeStruct((m, D_MODEL), ACT_DTYPE)]
    if emit_kv:
        out_specs += [pl.BlockSpec((1, 1, HEAD_W, lq), lambda b, h: (b, h, 0, 0)), head_out]
        out_shape += [jax.ShapeDtypeStruct((batch, ATTN_HEADS, HEAD_W, lq), F32),
                      jax.ShapeDtypeStruct((m, D_MODEL), F32)]
    scratch = [pltpu.VMEM((lq, HEAD_W), BF16),
               pltpu.VMEM((lq, HEAD_W), BF16),
               pltpu.VMEM((lq, HEAD_W), BF16),
               pltpu.VMEM((lq + lctx, 2 * ATTN_VD), BF16)]
    if lctx:
        scratch.append(pltpu.VMEM((HEAD_W, lctx), BF16))
    kern = functools.partial(_attn_kernel, lq=lq, lctx=lctx, tq=tq, lam_init=lam_init,
                             rope=rope_tabs is not None, emit_kv=emit_kv)
    return pl.pallas_call(
        kern,
        grid=(batch, ATTN_HEADS),
        in_specs=in_specs,
        out_specs=out_specs,
        out_shape=out_shape,
        scratch_shapes=scratch,
        compiler_params=_cparams(("arbitrary", "arbitrary")),
        name="attn",
    )(*args)


def _ssd_kernel(*refs, seq, has_h0):
    it = iter(refs)
    x_ref, b_ref, c_ref = next(it), next(it), next(it)
    cwx_ref, cwb_ref, cwc_ref = next(it), next(it), next(it)
    cbx_ref, cbb_ref, cbc_ref = next(it), next(it), next(it)
    dexp_ref, cs_ref, rtT_ref = next(it), next(it), next(it)
    if has_h0:
        h0f_ref, h0b_ref = next(it), next(it)
    y_ref, hf_ref, hb_ref = next(it), next(it), next(it)
    xm_s, bc_s, bT_s, cc_s, csg_s, stf, stb, yacc = (next(it) for _ in range(8))

    g = pl.program_id(1)
    nc = seq // CHUNK
    blk = lax.broadcasted_iota(jnp.int32, (CHUNK, GROUP_W), 1) // SSM_HEADDIM

    def conv(in_ref, w_ref, bias_ref, store):
        width = in_ref.shape[1]
        rows = CHUNK + 2 * CONV_PAD

        def body(c, carry):
            r0 = pl.multiple_of(c * CHUNK, CHUNK)
            p0 = pl.multiple_of(jnp.maximum(r0 - CONV_EDGE, 0), CONV_EDGE)
            n0 = pl.multiple_of(jnp.minimum(r0 + CHUNK, seq - CONV_EDGE), CONV_EDGE)
            prev = in_ref[pl.ds(p0, CONV_EDGE), :].astype(F32)[CONV_EDGE - CONV_PAD:, :]
            nxt = in_ref[pl.ds(n0, CONV_EDGE), :].astype(F32)[:CONV_PAD, :]
            win = jnp.concatenate([jnp.where(c > 0, prev, 0.0),
                                   in_ref[pl.ds(r0, CHUNK), :].astype(F32),
                                   jnp.where(c < nc - 1, nxt, 0.0)], axis=0)
            acc = jnp.broadcast_to(bias_ref[...], (CHUNK, width))
            for j in range(D_CONV):
                d = j - D_CONV // 2
                tap = win if d == 0 else pltpu.roll(win, (-d) % rows, axis=0)
                acc = acc + tap[CONV_PAD:CONV_PAD + CHUNK, :] * w_ref[j:j + 1, :]
            store(c, pl.ds(r0, CHUNK), _silu(acc))
            return carry

        lax.fori_loop(0, nc, body, 0, unroll=2)

    head_masks = [jnp.where(blk == j, 1.0, 0.0).astype(BF16) for j in range(HEADS_PER_GROUP)]

    def store_x(c, sl, val):
        yacc[sl, :] = dexp_ref[...] * val
        vb = val.astype(BF16)
        for j in range(HEADS_PER_GROUP):
            xm_s[c, j * CHUNK:(j + 1) * CHUNK, :] = vb * head_masks[j]

    def store_b(c, sl, val):
        bc_s[sl, :] = val.astype(BF16)
        bT_s[c] = val.T

    def store_c(c, sl, val):
        cc_s[sl, :] = val.astype(BF16)

    conv(x_ref, cwx_ref, cbx_ref, store_x)
    conv(b_ref, cwb_ref, cbb_ref, store_b)
    conv(c_ref, cwc_ref, cbc_ref, store_c)

    shift = (LANES - HEADS_PER_GROUP * g) % LANES
    csg_s[...] = pltpu.roll(cs_ref[...], shift, axis=1)

    if has_h0:
        stf[...] = h0f_ref[0].T
        stb[...] = h0b_ref[0].T
    else:
        stf[...] = jnp.zeros_like(stf)
        stb[...] = jnp.zeros_like(stb)

    row = lax.broadcasted_iota(jnp.int32, (CHUNK, CHUNK), 0)
    col = lax.broadcasted_iota(jnp.int32, (CHUNK, CHUNK), 1)
    lane_lo = lax.broadcasted_iota(jnp.int32, (CHUNK, LANES), 1) < SSM_HEADDIM

    def scan_chunk(c, forward):
        lane0 = 0 if forward else SSM_HEADS
        mask = (col <= row) if forward else (col >= row)
        edge = CHUNK - 1 if forward else 0
        st_ref = stf if forward else stb
        r0 = pl.multiple_of(c * CHUNK, CHUNK)
        sl = pl.ds(r0, CHUNK)
        cc = cc_s[sl, :]
        gmat = _dot_nt(cc, bc_s[sl, :])
        bT = bT_s[c]
        cs = csg_s[sl, :]
        diag_lhs, upd_lhs, cols = [], [], []
        for j in range(HEADS_PER_GROUP):
            cb = jnp.broadcast_to(cs[:, lane0 + j:lane0 + j + 1], (CHUNK, CHUNK))
            rrow = rtT_ref[c, pl.ds(lane0 + HEADS_PER_GROUP * g + j, 1), :]
            dec = jnp.exp(jnp.where(mask, cb - rrow, -jnp.inf))
            diag_lhs.append((gmat * dec).astype(BF16))
            w_edge = dec[edge:edge + 1, :]
            upd_lhs.append((bT * w_edge).astype(BF16))
            cols.append(cb)
        x_heads = xm_s[c]
        y = _dot(jnp.concatenate(diag_lhs, axis=1), x_heads)
        upd = _dot(jnp.concatenate(upd_lhs, axis=1), x_heads)
        cs_e = jnp.concatenate([jnp.where(lane_lo, cols[2 * t], cols[2 * t + 1])
                                for t in range(HEADS_PER_GROUP // 2)], axis=1)
        e = jnp.exp(cs_e)
        st = st_ref[...]
        yacc[sl, :] = yacc[sl, :] + y + e * _dot(cc, st.astype(BF16))
        st_ref[...] = e[edge:edge + 1, :] * st + upd

    def body(i, carry):
        scan_chunk(i, True)
        scan_chunk(nc - 1 - i, False)
        return carry

    lax.fori_loop(0, nc, body, 0, unroll=2)

    y_ref[...] = yacc[...].astype(y_ref.dtype)
    hf_ref[0] = stf[...].T
    hb_ref[0] = stb[...].T


def _ssd(p2d, cs, rtT, conv_w, conv_b, dexp, batch, seq, h0=None):
    m = p2d.shape[0]
    nc = seq // CHUNK
    xb, bb, cb = P_X // GROUP_W, P_B // SSM_STATE, P_C // SSM_STATE
    cw_b0 = D_INNER // SSM_STATE
    cw_c0 = cw_b0 + SSM_GROUPS
    in_specs = [pl.BlockSpec((seq, GROUP_W), lambda b, g: (b, xb + g)),
                pl.BlockSpec((seq, SSM_STATE), lambda b, g: (b, bb + g)),
                pl.BlockSpec((seq, SSM_STATE), lambda b, g: (b, cb + g)),
                pl.BlockSpec((D_CONV, GROUP_W), lambda b, g: (0, g)),
                pl.BlockSpec((D_CONV, SSM_STATE), lambda b, g: (0, cw_b0 + g)),
                pl.BlockSpec((D_CONV, SSM_STATE), lambda b, g: (0, cw_c0 + g)),
                pl.BlockSpec((1, GROUP_W), lambda b, g: (0, g)),
                pl.BlockSpec((1, SSM_STATE), lambda b, g: (0, cw_b0 + g)),
                pl.BlockSpec((1, SSM_STATE), lambda b, g: (0, cw_c0 + g)),
                pl.BlockSpec((1, GROUP_W), lambda b, g: (0, g)),
                pl.BlockSpec((seq, LANES), lambda b, g: (b, 0)),
                pl.BlockSpec((nc, CHUNK, CHUNK), lambda b, g: (b, 0, 0))]
    args = [p2d, p2d, p2d, conv_w, conv_w, conv_w, conv_b, conv_b, conv_b, dexp, cs, rtT]
    state_spec = pl.BlockSpec((1, GROUP_W, SSM_STATE), lambda b, g: (b, g, 0))
    if h0 is not None:
        in_specs += [state_spec, state_spec]
        args += list(h0)
    state_shape = jax.ShapeDtypeStruct((batch, D_INNER, SSM_STATE), F32)
    kern = functools.partial(_ssd_kernel, seq=seq, has_h0=h0 is not None)
    return pl.pallas_call(
        kern,
        grid=(batch, SSM_GROUPS),
        in_specs=in_specs,
        out_specs=[pl.BlockSpec((seq, GROUP_W), lambda b, g: (b, g)), state_spec, state_spec],
        out_shape=[jax.ShapeDtypeStruct((m, D_INNER), ACT_DTYPE), state_shape, state_shape],
        scratch_shapes=[pltpu.VMEM((nc, HEADS_PER_GROUP * CHUNK, GROUP_W), BF16),
                        pltpu.VMEM((seq, SSM_STATE), BF16),
                        pltpu.VMEM((nc, SSM_STATE, CHUNK), F32),
                        pltpu.VMEM((seq, SSM_STATE), BF16),
                        pltpu.VMEM((seq, LANES), F32),
                        pltpu.VMEM((SSM_STATE, GROUP_W), F32),
                        pltpu.VMEM((SSM_STATE, GROUP_W), F32),
                        pltpu.VMEM((seq, GROUP_W), F32)],
        compiler_params=_cparams(("arbitrary", "arbitrary")),
        name="ssd",
    )(*args)


def _merge_kernel(x_ref, mod_ref, o_ref, y_ref, z_ref, ga_ref, gb_ref, sg_ref,
                  wa_ref, wb_ref, wo_ref, out_ref):
    out_a = _dot(o_ref[...], wa_ref[...])
    yz = y_ref[...].astype(F32) * _silu(z_ref[...].astype(F32))
    yn = yz * lax.rsqrt(jnp.mean(yz * yz, axis=-1, keepdims=True) + EPS) * sg_ref[...]
    out_b = _dot(yn.astype(BF16), wb_ref[...])
    merged = (jax.nn.sigmoid(ga_ref[...].astype(F32)) * out_a
              + jax.nn.sigmoid(gb_ref[...].astype(F32)) * out_b)
    gate1 = mod_ref[0, :, 2 * D_MODEL:3 * D_MODEL]
    out_ref[...] = x_ref[...] + gate1 * _dot(merged.astype(BF16), wo_ref[...])


def _merge(x2d, mod, o, y, p2d, ssm_g, wa, wb, wo, tm):
    m = x2d.shape[0]
    rows_per_mod = m // mod.shape[0]
    return pl.pallas_call(
        _merge_kernel,
        grid=(m // tm,),
        in_specs=[pl.BlockSpec((tm, D_MODEL), lambda i: (i, 0)),
                  pl.BlockSpec((1, 1, 6 * D_MODEL), lambda i: ((i * tm) // rows_per_mod, 0, 0)),
                  pl.BlockSpec((tm, D_MODEL), lambda i: (i, 0)),
                  pl.BlockSpec((tm, D_INNER), lambda i: (i, 0)),
                  pl.BlockSpec((tm, D_INNER), lambda i: (i, P_Z // D_INNER)),
                  pl.BlockSpec((tm, D_MODEL), lambda i: (i, P_GA // D_MODEL)),
                  pl.BlockSpec((tm, D_MODEL), lambda i: (i, P_GB // D_MODEL)),
                  _resident((1, D_INNER)),
                  _resident((D_MODEL, D_MODEL)),
                  _resident((D_INNER, D_MODEL)),
                  _resident((D_MODEL, D_MODEL))],
        out_specs=pl.BlockSpec((tm, D_MODEL), lambda i: (i, 0)),
        out_shape=jax.ShapeDtypeStruct((m, D_MODEL), F32),
        compiler_params=_cparams(("arbitrary",)),
        name="merge",
    )(x2d, mod, o, y, p2d, p2d, p2d, ssm_g, wa, wb, wo)


def _ffn_kernel(x_ref, mod_ref, g_ref, wg_ref, wu_ref, wd_ref, out_ref):
    x = x_ref[...]
    y = x * lax.rsqrt(jnp.mean(x * x, axis=-1, keepdims=True) + EPS) * g_ref[...]
    shift = mod_ref[0, :, 3 * D_MODEL:4 * D_MODEL]
    scale = mod_ref[0, :, 4 * D_MODEL:5 * D_MODEL]
    gate = mod_ref[0, :, 5 * D_MODEL:6 * D_MODEL]
    h = (y * (1.0 + scale) + shift).astype(BF16)
    f = (_silu(_dot(h, wg_ref[...])) * _dot(h, wu_ref[...])).astype(BF16)
    out_ref[...] = x + gate * _dot(f, wd_ref[...])


def _ffn(x2d, mod, g2, wg, wu, wd, tm):
    m = x2d.shape[0]
    rows_per_mod = m // mod.shape[0]
    return pl.pallas_call(
        _ffn_kernel,
        grid=(m // tm,),
        in_specs=[pl.BlockSpec((tm, D_MODEL), lambda i: (i, 0)),
                  pl.BlockSpec((1, 1, 6 * D_MODEL), lambda i: ((i * tm) // rows_per_mod, 0, 0)),
                  _resident((1, D_MODEL)),
                  _resident((D_MODEL, D_FF)),
                  _resident((D_MODEL, D_FF)),
                  _resident((D_FF, D_MODEL))],
        out_specs=pl.BlockSpec((tm, D_MODEL), lambda i: (i, 0)),
        out_shape=jax.ShapeDtypeStruct((m, D_MODEL), F32),
        compiler_params=_cparams(("arbitrary",)),
        name="ffn",
    )(x2d, mod, g2, wg, wu, wd)


def _rope_tables(n_tok):
    pos = jnp.arange(n_tok, dtype=jnp.int32)
    row = (pos // GRID_W).astype(F32)
    colp = (pos % GRID_W).astype(F32)
    inv = ROPE_BASE ** (-jnp.arange(0, ROPE_AXIS_DIM, 2, dtype=F32) / ROPE_AXIS_DIM)
    ang_r = row[:, None] * inv[None, :]
    ang_c = colp[:, None] * inv[None, :]
    zero = jnp.zeros_like(ang_r)
    cos = jnp.concatenate([jnp.cos(ang_r)] * 2 + [jnp.cos(ang_c)] * 2, axis=-1)
    sup = jnp.concatenate([-jnp.sin(ang_r), zero, -jnp.sin(ang_c), zero], axis=-1)
    sdn = jnp.concatenate([zero, jnp.sin(ang_r), zero, jnp.sin(ang_c)], axis=-1)
    return tuple(jnp.tile(t, (1, 2)) for t in (cos, sup, sdn))


def _pad_lanes(v):
    flat = v.reshape(1, -1).astype(F32)
    return jnp.pad(flat, ((0, 0), (0, LANES - flat.shape[1])))


def kernel(x_prompt, x_sample, c, cache_k, cache_v, state_ssm_fwd, state_ssm_bwd, c_ctx, norm1_g, norm2_g, w_ada, b_ada, w_in, q_norm_g, k_norm_g, lambda_q1, lambda_k1, lambda_q2, lambda_k2, attn_sub_g, conv_w, conv_b, A_log, dt_bias, D_skip, ssm_norm_g, w_branch_a, w_branch_b, w_out, w_ffn_gate, w_ffn_up, w_ffn_down):
    depth = norm1_g.shape[0]
    assert depth == 1, "single-layer kernel"
    l = 0
    lam_init = 0.8 - 0.6 * math.exp(-0.3 * l)
    nb, seq, _ = x_prompt.shape
    db, dseq, _ = x_sample.shape

    cond = jnp.concatenate([c_ctx[None, :], c, jnp.zeros((SUBLANES - 1 - db, D_MODEL), F32)], axis=0)
    mod = _ada(cond, w_ada[l], b_ada[l][None, :])
    mod_ctx = mod[0:1].reshape(1, 1, 6 * D_MODEL)
    mod_lat = mod[1:1 + db].reshape(db, 1, 6 * D_MODEL)

    w = jnp.transpose(w_in[l])
    w_gates = w[W_G0:W_G0 + 2 * D_MODEL]
    w_dt = jnp.pad(w[W_DT0:W_G0], ((0, LANES - 2 * SSM_HEADS), (0, 0)))
    wa = w_branch_a[l].astype(BF16)
    wb = w_branch_b[l].astype(BF16)
    wo = w_out[l].astype(BF16)
    wg = w_ffn_gate[l].astype(BF16)
    wu = w_ffn_up[l].astype(BF16)
    wd = w_ffn_down[l].astype(BF16)
    g1 = norm1_g[l][None, :]
    g2 = norm2_g[l][None, :]
    qg = jnp.tile(q_norm_g[l], 2)[None, :]
    kg = jnp.tile(k_norm_g[l], 2)[None, :]
    sg = attn_sub_g[l][None, :]
    lamp = jnp.stack([lambda_q1[l], lambda_k1[l], lambda_q2[l], lambda_k2[l]], axis=0)
    bias_row = _pad_lanes(dt_bias[l])
    alog_row = _pad_lanes(A_log[l])
    dexp = jnp.repeat(D_skip[l], SSM_HEADDIM)[None, :]
    cw = conv_w[l]
    cb = conv_b[l][None, :]
    ssm_g = ssm_norm_g[l][None, :]

    def layer(x, mod_rows, rope_tabs, ctx, h0, emit_kv):
        batch, sl, _ = x.shape
        x2d = x.reshape(batch * sl, D_MODEL)
        h, dt_raw = _norm1(x2d, mod_rows, g1, w_dt, tm=512)
        p2d = _proj(h, w, w_gates, tm=1024)
        cs, rtT = _dtprep(dt_raw, bias_row, alog_row)
        attn_out = _attn(p2d, lamp, qg, kg, sg, batch, sl, lam_init, rope_tabs, ctx, emit_kv)
        y, hf, hb = _ssd(p2d, cs, rtT, cw, cb, dexp, batch, sl, h0)
        x1 = _merge(x2d, mod_rows, attn_out[0], y, p2d, ssm_g, wa, wb, wo, tm=512)
        x2 = _ffn(x1, mod_rows, g2, wg, wu, wd, tm=512)
        return x2.reshape(batch, sl, D_MODEL), attn_out[1:], hf, hb

    y_prompt, kv, hf, hb = layer(x_prompt, mod_ctx, None, None, None, True)
    ctx_kT = jnp.transpose(cache_k[:, l], (0, 2, 3, 4, 1)).reshape(db, ATTN_HEADS, HEAD_W, -1)
    ctx_v = cache_v[:, l].reshape(db, -1, ATTN_VD)
    h0 = (state_ssm_fwd[:, l].reshape(db, D_INNER, SSM_STATE),
          state_ssm_bwd[:, l].reshape(db, D_INNER, SSM_STATE))
    y_sample, _, _, _ = layer(x_sample, mod_lat, _rope_tables(dseq), (ctx_kT, ctx_v), h0, False)

    knT = kv[0].reshape(nb, ATTN_HEADS, 2, ATTN_DH, seq)
    new_cache_k = jnp.transpose(knT, (0, 4, 1, 2, 3)).reshape(nb, 1, seq, ATTN_HEADS, 2, ATTN_DH)
    new_cache_v = kv[1].reshape(nb, 1, seq, ATTN_HEADS, ATTN_VD)
    new_hf = hf.reshape(nb, 1, SSM_HEADS, SSM_HEADDIM, SSM_STATE)
    new_hb = hb.reshape(nb, 1, SSM_HEADS, SSM_HEADDIM, SSM_STATE)
    return (y_prompt, y_sample, new_cache_k, new_cache_v, new_hf, new_hb)
```

```python
import functools
import math

import jax
import jax.numpy as jnp
from jax import lax
from jax.experimental import pallas as pl
from jax.experimental.pallas import tpu as pltpu

F32 = jnp.float32
BF16 = jnp.bfloat16

D_MODEL = 1024
GRID_W = 64
ATTN_HEADS = 8
ATTN_DH = 64
ATTN_VD = 128
HEAD_W = 2 * ATTN_DH
ROPE_BASE = 10000.0
ROPE_AXIS_DIM = ATTN_DH // 2
ROPE_HALF = ROPE_AXIS_DIM // 2
D_INNER = 2 * D_MODEL
SSM_HEADDIM = 64
SSM_HEADS = D_INNER // SSM_HEADDIM
SSM_GROUPS = 8
HEADS_PER_GROUP = SSM_HEADS // SSM_GROUPS
GROUP_W = HEADS_PER_GROUP * SSM_HEADDIM
SSM_STATE = 128
D_CONV = 5
CONV_PAD = 8
CONV_EDGE = 16
CHUNK = 128
D_FF = ((8 * D_MODEL // 3 + 255) // 256) * 256
EPS = 1e-6
LANES = 128
SUBLANES = 8
LOG2E = math.log2(math.e)

P_Z = 0
P_GA = P_Z + D_INNER
P_GB = P_GA + D_MODEL
P_X = P_GB + D_MODEL
P_B = P_X + D_INNER
P_C = P_B + SSM_GROUPS * SSM_STATE
P_Q = P_C + SSM_GROUPS * SSM_STATE
P_K = P_Q + D_MODEL
P_V = P_K + D_MODEL
NP = P_V + D_MODEL
PROJ_TN = 1024
ACT_DTYPE = BF16

W_Q0 = 0
W_Z0 = 3 * D_MODEL
W_XBC0 = W_Z0 + D_INNER
W_DT0 = W_XBC0 + D_INNER + 2 * SSM_GROUPS * SSM_STATE
W_G0 = W_DT0 + 2 * SSM_HEADS

VMEM_LIMIT = 56 * 1024 * 1024


def _cparams(sem):
    return pltpu.CompilerParams(dimension_semantics=sem, vmem_limit_bytes=VMEM_LIMIT)


def _silu(x):
    return x * jax.nn.sigmoid(x)


def _dot(a, b):
    return jnp.dot(a, b, preferred_element_type=F32)


def _dot_nt(a, b):
    return lax.dot_general(a, b, (((1,), (1,)), ((), ())), preferred_element_type=F32)


def _split3(x):
    hi = x.astype(BF16)
    r = x - hi.astype(F32)
    mid = r.astype(BF16)
    lo = (r - mid.astype(F32)).astype(BF16)
    return hi, mid, lo


def _resident(shape):
    return pl.BlockSpec(shape, lambda *_: (0,) * len(shape), pipeline_mode=pl.Buffered(1))


def _ada_kernel(cond_ref, w_ref, b_ref, o_ref):
    s = _silu(cond_ref[...]).astype(BF16)
    o_ref[...] = _dot(s, w_ref[...].astype(BF16)) + b_ref[...]


def _ada(cond, w_ada, b_ada):
    rows = cond.shape[0]
    n = w_ada.shape[1]
    tn = 1024
    return pl.pallas_call(
        _ada_kernel,
        grid=(n // tn,),
        in_specs=[pl.BlockSpec((rows, D_MODEL), lambda j: (0, 0)),
                  pl.BlockSpec((D_MODEL, tn), lambda j: (0, j)),
                  pl.BlockSpec((1, tn), lambda j: (0, j))],
        out_specs=pl.BlockSpec((rows, tn), lambda j: (0, j)),
        out_shape=jax.ShapeDtypeStruct((rows, n), F32),
        compiler_params=_cparams(("arbitrary",)),
        name="ada",
    )(cond, w_ada, b_ada)


def _norm1_kernel(x_ref, mod_ref, g_ref, wdt_ref, h_ref, dt_ref):
    x = x_ref[...]
    y = x * lax.rsqrt(jnp.mean(x * x, axis=-1, keepdims=True) + EPS) * g_ref[...]
    shift = mod_ref[0, :, 0:D_MODEL]
    scale = mod_ref[0, :, D_MODEL:2 * D_MODEL]
    h = (y * (1.0 + scale) + shift).astype(BF16)
    h_ref[...] = h
    dt_ref[...] = _dot_nt(h, wdt_ref[...].astype(BF16))


def _norm1(x2d, mod, g1, w_dt, tm):
    m = x2d.shape[0]
    rows_per_mod = m // mod.shape[0]
    return pl.pallas_call(
        _norm1_kernel,
        grid=(m // tm,),
        in_specs=[pl.BlockSpec((tm, D_MODEL), lambda i: (i, 0)),
                  pl.BlockSpec((1, 1, 6 * D_MODEL), lambda i: ((i * tm) // rows_per_mod, 0, 0)),
                  _resident((1, D_MODEL)),
                  _resident((LANES, D_MODEL))],
        out_specs=[pl.BlockSpec((tm, D_MODEL), lambda i: (i, 0)),
                   pl.BlockSpec((tm, LANES), lambda i: (i, 0))],
        out_shape=[jax.ShapeDtypeStruct((m, D_MODEL), BF16),
                   jax.ShapeDtypeStruct((m, LANES), F32)],
        compiler_params=_cparams(("arbitrary",)),
        name="norm1",
    )(x2d, mod, g1, w_dt)


_GATE_TILE0 = P_GA // PROJ_TN
_N_GATE_TILES = 2 * D_MODEL // PROJ_TN


def _w_in_tile(j):
    z_tiles = D_INNER // PROJ_TN
    xbc_tile0 = (P_X // PROJ_TN)
    qkv_tile0 = (P_Q // PROJ_TN)
    z_src = jnp.minimum(j, z_tiles - 1) + W_Z0 // PROJ_TN
    xbc_src = j - xbc_tile0 + W_XBC0 // PROJ_TN
    qkv_src = j - qkv_tile0 + W_Q0 // PROJ_TN
    return jnp.where(j < xbc_tile0, z_src, jnp.where(j < qkv_tile0, xbc_src, qkv_src))


def _proj_kernel(h_ref, win_ref, wg_ref, p_ref, w_scr, *, tm):
    j = pl.program_id(0)
    i = pl.program_id(1)
    is_gate = jnp.logical_and(j >= _GATE_TILE0, j < _GATE_TILE0 + _N_GATE_TILES)

    @pl.when(jnp.logical_and(i == 0, is_gate))
    def _():
        w_scr[...] = wg_ref[...].astype(BF16)

    @pl.when(jnp.logical_and(i == 0, jnp.logical_not(is_gate)))
    def _():
        w_scr[...] = win_ref[...].astype(BF16)

    r0 = pl.multiple_of(i * tm, tm)
    p_ref[...] = _dot_nt(h_ref[pl.ds(r0, tm), :], w_scr[...]).astype(p_ref.dtype)


def _proj(h, w_in_t, w_gates_t, tm):
    m = h.shape[0]
    tn = PROJ_TN
    return pl.pallas_call(
        functools.partial(_proj_kernel, tm=tm),
        grid=(NP // tn, m // tm),
        in_specs=[_resident((m, D_MODEL)),
                  pl.BlockSpec((tn, D_MODEL), lambda j, i: (_w_in_tile(j), 0)),
                  pl.BlockSpec((tn, D_MODEL),
                               lambda j, i: (jnp.clip(j - _GATE_TILE0, 0, _N_GATE_TILES - 1), 0))],
        out_specs=pl.BlockSpec((tm, tn), lambda j, i: (i, j)),
        out_shape=jax.ShapeDtypeStruct((m, NP), ACT_DTYPE),
        scratch_shapes=[pltpu.VMEM((tn, D_MODEL), BF16)],
        compiler_params=_cparams(("arbitrary", "arbitrary")),
        name="proj",
    )(h, w_in_t, w_gates_t)


DTPREP_CHUNKS = 4
SSD_GROUPS_PER_STEP = 2


def _dtprep_kernel(dtr_ref, bias_ref, alog_ref, cs_ref, rtT_ref):
    row = lax.broadcasted_iota(jnp.int32, (CHUNK, CHUNK), 0)
    col = lax.broadcasted_iota(jnp.int32, (CHUNK, CHUNK), 1)
    tril = jnp.where(col <= row, 1.0, 0.0).astype(BF16)
    triu = jnp.where(col >= row, 1.0, 0.0).astype(BF16)
    neg_a = -jnp.exp(alog_ref[...])
    for c in range(DTPREP_CHUNKS):
        sl = slice(c * CHUNK, (c + 1) * CHUNK)
        x = dtr_ref[sl, :] + bias_ref[...]
        dt = jnp.maximum(x, 0.0) + jnp.log1p(jnp.exp(-jnp.abs(x)))
        hi, mid, lo = _split3(dt * neg_a)
        cf = _dot(tril, hi) + _dot(tril, mid) + _dot(tril, lo)
        cb = _dot(triu, hi) + _dot(triu, mid) + _dot(triu, lo)
        cs = jnp.where(col < SSM_HEADS, cf, cb)
        cs_ref[sl, :] = cs * LOG2E
        rtT_ref[c] = ((cs - jnp.log(dt)) * LOG2E).T


def _dtprep(dt_raw, bias_row, alog_row):
    m = dt_raw.shape[0]
    rows = DTPREP_CHUNKS * CHUNK
    return pl.pallas_call(
        _dtprep_kernel,
        grid=(m // rows,),
        in_specs=[pl.BlockSpec((rows, LANES), lambda i: (i, 0)),
                  _resident((1, LANES)),
                  _resident((1, LANES))],
        out_specs=[pl.BlockSpec((rows, LANES), lambda i: (i, 0)),
                   pl.BlockSpec((DTPREP_CHUNKS, CHUNK, CHUNK), lambda i: (i, 0, 0))],
        out_shape=[jax.ShapeDtypeStruct((m, LANES), F32),
                   jax.ShapeDtypeStruct((m // CHUNK, CHUNK, CHUNK), F32)],
        compiler_params=_cparams(("arbitrary",)),
        name="dtprep",
    )(dt_raw, bias_row, alog_row)


def _half_norm(x, g, half_ones):
    ss = x * x
    hi = ss.astype(BF16)
    lo = (ss - hi.astype(F32)).astype(BF16)
    ms = (_dot(hi, half_ones) + _dot(lo, half_ones)) * (1.0 / ATTN_DH)
    return x * lax.rsqrt(ms + EPS) * g


def _rope(x, cos, sin_up, sin_dn):
    x_up = pltpu.roll(x, HEAD_W - ROPE_HALF, axis=1)
    x_dn = pltpu.roll(x, ROPE_HALF, axis=1)
    return x * cos + x_up * sin_up + x_dn * sin_dn


def _attn_kernel(*refs, lq, lctx, hps, lam_init, rope, emit_kv):
    it = iter(refs)
    lamp_ref, qg_ref, kg_ref, sg_ref, q_ref, k_ref, v_ref = (next(it) for _ in range(7))
    if rope:
        cos_ref, sup_ref, sdn_ref = next(it), next(it), next(it)
    if lctx:
        ckT_ref, cv_ref = next(it), next(it)
    o_ref = next(it)
    if emit_kv:
        knT_ref, vo_ref = next(it), next(it)
    head_scr = [tuple(next(it) for _ in range(4)) for _ in range(hps)]
    if lctx:
        kcT_scrs = [next(it) for _ in range(hps)]

    lp = lamp_ref[...]
    lam = (jnp.exp(jnp.sum(lp[0:1] * lp[1:2], axis=-1, keepdims=True))
           - jnp.exp(jnp.sum(lp[2:3] * lp[3:4], axis=-1, keepdims=True)) + lam_init)

    r_i = lax.broadcasted_iota(jnp.int32, (HEAD_W, HEAD_W), 0) // ATTN_DH
    c_i = lax.broadcasted_iota(jnp.int32, (HEAD_W, HEAD_W), 1) // ATTN_DH
    half_ones = jnp.where(r_i == c_i, 1.0, 0.0).astype(BF16)

    first_half = lax.broadcasted_iota(jnp.int32, (lq, HEAD_W), 1) < ATTN_DH
    sub_gain = sg_ref[...] * (1.0 - lam_init)
    ones = jnp.ones((lq + lctx, ATTN_VD), BF16)

    def stage_ctx_values(va_scr, hh):
        head = pl.program_id(1) * hps + hh
        rows = cv_ref[0, pl.ds(head, lctx, stride=ATTN_HEADS), :]
        va_scr[lq:lq + lctx, 0:ATTN_VD] = rows.astype(BF16)

    for hh in range(hps):
        cols = slice(hh * HEAD_W, (hh + 1) * HEAD_W)
        qn = _half_norm(q_ref[:, cols].astype(F32), qg_ref[...], half_ones)
        kn = _half_norm(k_ref[:, cols].astype(F32), kg_ref[...], half_ones)
        v = v_ref[:, cols]
        if emit_kv:
            knT_ref[0, hh] = kn.T
            vo_ref[:, cols] = v.astype(F32)
        if rope:
            qn = _rope(qn, cos_ref[...], sup_ref[...], sdn_ref[...])
            kn = _rope(kn, cos_ref[...], sup_ref[...], sdn_ref[...])
        qs = qn * (ATTN_DH ** -0.5 * LOG2E)
        q1_scr, q2_scr, kk_scr, va_scr = head_scr[hh]
        q1_scr[...] = jnp.where(first_half, qs, 0.0).astype(BF16)
        q2_scr[...] = jnp.where(first_half, 0.0, qs).astype(BF16)
        kk_scr[...] = kn.astype(BF16)
        va_scr[0:lq, 0:ATTN_VD] = v.astype(BF16)
        va_scr[:, ATTN_VD:2 * ATTN_VD] = ones
        if lctx:
            kcT_scrs[hh][...] = ckT_ref[0, hh].astype(BF16)
            stage_ctx_values(va_scr, hh)

        def softmax_pv(qb, kk_scr=kk_scr, va_scr=va_scr, hh=hh):
            s = _dot_nt(qb, kk_scr[...])
            if lctx:
                s = jnp.concatenate([s, _dot(qb, kcT_scrs[hh][...])], axis=1)
            p = jnp.exp2(s - jnp.max(s, axis=-1, keepdims=True)).astype(BF16)
            a = _dot(p, va_scr[...])
            return a[:, 0:ATTN_VD] / a[:, ATTN_VD:2 * ATTN_VD]

        o = softmax_pv(q1_scr[...]) - lam * softmax_pv(q2_scr[...])
        o = o * lax.rsqrt(jnp.mean(o * o, axis=-1, keepdims=True) + EPS) * sub_gain
        o_ref[:, cols] = o.astype(o_ref.dtype)


def _attn(p2d, lamp, qg, kg, sg, batch, lq, lam_init, hps, rope_tabs=None, ctx=None, emit_kv=False):
    m = p2d.shape[0]
    lctx = 0 if ctx is None else ctx[0].shape[-1]
    width = hps * HEAD_W
    qb, kb, vb = P_Q // width, P_K // width, P_V // width
    const = lambda b, h: (0, 0)
    in_specs = [pl.BlockSpec((4, ATTN_DH), const),
                pl.BlockSpec((1, HEAD_W), const),
                pl.BlockSpec((1, HEAD_W), const),
                pl.BlockSpec((1, ATTN_VD), const),
                pl.BlockSpec((lq, width), lambda b, h: (b, qb + h)),
                pl.BlockSpec((lq, width), lambda b, h: (b, kb + h)),
                pl.BlockSpec((lq, width), lambda b, h: (b, vb + h))]
    args = [lamp, qg, kg, sg, p2d, p2d, p2d]
    if rope_tabs is not None:
        in_specs += [pl.BlockSpec((lq, HEAD_W), const)] * 3
        args += list(rope_tabs)
    if ctx is not None:
        in_specs += [pl.BlockSpec((1, hps, HEAD_W, lctx), lambda b, h: (b, h, 0, 0)),
                     pl.BlockSpec((1, lctx * ATTN_HEADS, ATTN_VD), lambda b, h: (b, 0, 0))]
        args += list(ctx)
    head_out = pl.BlockSpec((lq, width), lambda b, h: (b, h))
    out_specs = [head_out]
    out_shape = [jax.ShapeDtypeStruct((m, D_MODEL), ACT_DTYPE)]
    if emit_kv:
        out_specs += [pl.BlockSpec((1, hps, HEAD_W, lq), lambda b, h: (b, h, 0, 0)), head_out]
        out_shape += [jax.ShapeDtypeStruct((batch, ATTN_HEADS, HEAD_W, lq), F32),
                      jax.ShapeDtypeStruct((m, D_MODEL), F32)]
    scratch = hps * [pltpu.VMEM((lq, HEAD_W), BF16),
                     pltpu.VMEM((lq, HEAD_W), BF16),
                     pltpu.VMEM((lq, HEAD_W), BF16),
                     pltpu.VMEM((lq + lctx, 2 * ATTN_VD), BF16)]
    if lctx:
        scratch += hps * [pltpu.VMEM((HEAD_W, lctx), BF16)]
    kern = functools.partial(_attn_kernel, lq=lq, lctx=lctx, hps=hps, lam_init=lam_init,
                             rope=rope_tabs is not None, emit_kv=emit_kv)
    return pl.pallas_call(
        kern,
        grid=(batch, ATTN_HEADS // hps),
        in_specs=in_specs,
        out_specs=out_specs,
        out_shape=out_shape,
        scratch_shapes=scratch,
        compiler_params=_cparams(("arbitrary", "arbitrary")),
        name="attn",
    )(*args)


def _ssd_kernel(*refs, seq, has_h0, gps):
    it = iter(refs)
    x_ref, b_ref, c_ref = next(it), next(it), next(it)
    cwx_ref, cwb_ref, cwc_ref = next(it), next(it), next(it)
    cbx_ref, cbb_ref, cbc_ref = next(it), next(it), next(it)
    dexp_ref, cs_ref, rtT_ref = next(it), next(it), next(it)
    if has_h0:
        h0f_ref, h0b_ref = next(it), next(it)
    y_ref, hf_ref, hb_ref = next(it), next(it), next(it)
    group_scr = [tuple(next(it) for _ in range(7)) for _ in range(gps)]

    g0 = pl.program_id(1) * gps
    nc = seq // CHUNK
    x_lanes = [slice(gi * GROUP_W, (gi + 1) * GROUP_W) for gi in range(gps)]
    n_lanes = [slice(gi * SSM_STATE, (gi + 1) * SSM_STATE) for gi in range(gps)]
    blk = lax.broadcasted_iota(jnp.int32, (CHUNK, GROUP_W), 1) // SSM_HEADDIM

    win_rows = CHUNK + 2 * CONV_EDGE
    sel_r = lax.broadcasted_iota(jnp.int32, (CHUNK, win_rows), 0)
    sel_c = lax.broadcasted_iota(jnp.int32, (CHUNK, win_rows), 1)
    taps = [d for d in range(-(D_CONV // 2), D_CONV // 2 + 1) if d != 0]
    shift_sel = {d: jnp.where(sel_c == sel_r + CONV_EDGE + d, 1.0, 0.0).astype(BF16) for d in taps}

    def conv_chunk(c, in_ref, w_ref, bias_ref, lanes, shift_on_mxu):
        assert in_ref.dtype == BF16
        r0 = pl.multiple_of(c * CHUNK, CHUNK)
        p0 = pl.multiple_of(jnp.maximum(r0 - CONV_EDGE, 0), CONV_EDGE)
        n0 = pl.multiple_of(jnp.minimum(r0 + CHUNK, seq - CONV_EDGE), CONV_EDGE)
        prev = in_ref[pl.ds(p0, CONV_EDGE), lanes]
        nxt = in_ref[pl.ds(n0, CONV_EDGE), lanes]
        main = in_ref[pl.ds(r0, CHUNK), lanes]
        w = [w_ref[j:j + 1, lanes] for j in range(D_CONV)]
        acc = bias_ref[:, lanes] + main.astype(F32) * w[D_CONV // 2]
        if shift_on_mxu:
            zero = jnp.zeros_like(prev)
            win = jnp.concatenate([jnp.where(c > 0, prev, zero), main,
                                   jnp.where(c < nc - 1, nxt, zero)], axis=0)
            for d in taps:
                acc = acc + _dot(shift_sel[d], win) * w[d + D_CONV // 2]
        else:
            rows = CHUNK + 2 * CONV_PAD
            win = jnp.concatenate(
                [jnp.where(c > 0, prev.astype(F32)[CONV_EDGE - CONV_PAD:, :], 0.0), main.astype(F32),
                 jnp.where(c < nc - 1, nxt.astype(F32)[:CONV_PAD, :], 0.0)], axis=0)
            for d in taps:
                tap = pltpu.roll(win, (-d) % rows, axis=0)[CONV_PAD:CONV_PAD + CHUNK, :]
                acc = acc + tap * w[d + D_CONV // 2]
        return _silu(acc)

    head_masks = [jnp.where(blk == j, 1.0, 0.0).astype(BF16) for j in range(HEADS_PER_GROUP)]

    def conv_body(c, carry):
        sl = pl.ds(pl.multiple_of(c * CHUNK, CHUNK), CHUNK)
        for gi in range(gps):
            xm_s, bT_s, cc_s, _, _, _, yacc = group_scr[gi]
            xv = conv_chunk(c, x_ref, cwx_ref, cbx_ref, x_lanes[gi], nc > 2)
            yacc[sl, :] = dexp_ref[:, x_lanes[gi]] * xv
            xb = xv.astype(BF16)
            for j in range(HEADS_PER_GROUP):
                xm_s[c, j * CHUNK:(j + 1) * CHUNK, :] = xb * head_masks[j]
            bT_s[c] = conv_chunk(c, b_ref, cwb_ref, cbb_ref, n_lanes[gi], False).T.astype(BF16)
            cc_s[sl, :] = conv_chunk(c, c_ref, cwc_ref, cbc_ref, n_lanes[gi], False).astype(BF16)
        return carry

    lax.fori_loop(0, nc, conv_body, 0, unroll=2)

    for gi in range(gps):
        _, _, _, csg_s, stf, stb, _ = group_scr[gi]
        shift = (LANES - HEADS_PER_GROUP * (g0 + gi)) % LANES
        csg_s[...] = pltpu.roll(cs_ref[...], shift, axis=1)
        if has_h0:
            stf[...] = h0f_ref[0, x_lanes[gi], :].T
            stb[...] = h0b_ref[0, x_lanes[gi], :].T
        else:
            stf[...] = jnp.zeros_like(stf)
            stb[...] = jnp.zeros_like(stb)

    row = lax.broadcasted_iota(jnp.int32, (CHUNK, CHUNK), 0)
    col = lax.broadcasted_iota(jnp.int32, (CHUNK, CHUNK), 1)
    lane_lo = lax.broadcasted_iota(jnp.int32, (CHUNK, LANES), 1) < SSM_HEADDIM

    def scan_chunk(c, forward, gi):
        xm_s, bT_s, cc_s, csg_s, stf, stb, yacc = group_scr[gi]
        g = g0 + gi
        lane0 = 0 if forward else SSM_HEADS
        mask = (col <= row) if forward else (col >= row)
        edge = CHUNK - 1 if forward else 0
        st_ref = stf if forward else stb
        r0 = pl.multiple_of(c * CHUNK, CHUNK)
        sl = pl.ds(r0, CHUNK)
        cc = cc_s[sl, :]
        bT = bT_s[c]
        gmat = _dot(cc, bT)
        cs = csg_s[sl, :]
        diag_lhs, upd_lhs, cols = [], [], []
        for j in range(HEADS_PER_GROUP):
            cb = jnp.broadcast_to(cs[:, lane0 + j:lane0 + j + 1], (CHUNK, CHUNK))
            rrow = rtT_ref[c, pl.ds(lane0 + HEADS_PER_GROUP * g + j, 1), :]
            dec = jnp.exp2(jnp.where(mask, cb - rrow, -jnp.inf))
            diag_lhs.append((gmat * dec).astype(BF16))
            w_edge = dec[edge:edge + 1, :]
            upd_lhs.append(bT * w_edge.astype(BF16))
            cols.append(cb)
        x_heads = xm_s[c]
        y = _dot(jnp.concatenate(diag_lhs, axis=1), x_heads)
        upd = _dot(jnp.concatenate(upd_lhs, axis=1), x_heads)
        cs_e = jnp.concatenate([jnp.where(lane_lo, cols[2 * t], cols[2 * t + 1])
                                for t in range(HEADS_PER_GROUP // 2)], axis=1)
        e = jnp.exp2(cs_e)
        st = st_ref[...]
        yacc[sl, :] = yacc[sl, :] + y + e * _dot(cc, st.astype(BF16))
        st_ref[...] = e[edge:edge + 1, :] * st + upd

    def body(i, carry):
        for gi in range(gps):
            scan_chunk(i, True, gi)
            scan_chunk(nc - 1 - i, False, gi)
        return carry

    lax.fori_loop(0, nc, body, 0, unroll=2)

    for gi in range(gps):
        _, _, _, _, stf, stb, yacc = group_scr[gi]
        y_ref[:, x_lanes[gi]] = yacc[...].astype(y_ref.dtype)
        hf_ref[0, x_lanes[gi], :] = stf[...].T
        hb_ref[0, x_lanes[gi], :] = stb[...].T


def _ssd(p2d, cs, rtT, conv_w, conv_b, dexp, batch, seq, gps, h0=None):
    m = p2d.shape[0]
    nc = seq // CHUNK
    xw, nw = gps * GROUP_W, gps * SSM_STATE
    xb, bb, cb = P_X // xw, P_B // nw, P_C // nw
    cw_b0 = D_INNER // nw
    cw_c0 = cw_b0 + SSM_GROUPS // gps
    in_specs = [pl.BlockSpec((seq, xw), lambda b, g: (b, xb + g)),
                pl.BlockSpec((seq, nw), lambda b, g: (b, bb + g)),
                pl.BlockSpec((seq, nw), lambda b, g: (b, cb + g)),
                pl.BlockSpec((D_CONV, xw), lambda b, g: (0, g)),
                pl.BlockSpec((D_CONV, nw), lambda b, g: (0, cw_b0 + g)),
                pl.BlockSpec((D_CONV, nw), lambda b, g: (0, cw_c0 + g)),
                pl.BlockSpec((1, xw), lambda b, g: (0, g)),
                pl.BlockSpec((1, nw), lambda b, g: (0, cw_b0 + g)),
                pl.BlockSpec((1, nw), lambda b, g: (0, cw_c0 + g)),
                pl.BlockSpec((1, xw), lambda b, g: (0, g)),
                pl.BlockSpec((seq, LANES), lambda b, g: (b, 0)),
                pl.BlockSpec((nc, CHUNK, CHUNK), lambda b, g: (b, 0, 0))]
    args = [p2d, p2d, p2d, conv_w, conv_w, conv_w, conv_b, conv_b, conv_b, dexp, cs, rtT]
    state_spec = pl.BlockSpec((1, xw, SSM_STATE), lambda b, g: (b, g, 0))
    if h0 is not None:
        in_specs += [state_spec, state_spec]
        args += list(h0)
    state_shape = jax.ShapeDtypeStruct((batch, D_INNER, SSM_STATE), F32)
    kern = functools.partial(_ssd_kernel, seq=seq, has_h0=h0 is not None, gps=gps)
    return pl.pallas_call(
        kern,
        grid=(batch, SSM_GROUPS // gps),
        in_specs=in_specs,
        out_specs=[pl.BlockSpec((seq, xw), lambda b, g: (b, g)), state_spec, state_spec],
        out_shape=[jax.ShapeDtypeStruct((m, D_INNER), ACT_DTYPE), state_shape, state_shape],
        scratch_shapes=gps * [pltpu.VMEM((nc, HEADS_PER_GROUP * CHUNK, GROUP_W), BF16),
                              pltpu.VMEM((nc, SSM_STATE, CHUNK), BF16),
                              pltpu.VMEM((seq, SSM_STATE), BF16),
                              pltpu.VMEM((seq, LANES), F32),
                              pltpu.VMEM((SSM_STATE, GROUP_W), F32),
                              pltpu.VMEM((SSM_STATE, GROUP_W), F32),
                              pltpu.VMEM((seq, GROUP_W), F32)],
        compiler_params=_cparams(("arbitrary", "arbitrary")),
        name="ssd",
    )(*args)


def _merge_kernel(x_ref, mod_ref, o_ref, y_ref, z_ref, ga_ref, gb_ref, sg_ref,
                  wa_ref, wb_ref, wo_ref, out_ref):
    out_a = _dot(o_ref[...], wa_ref[...])
    yz = y_ref[...].astype(F32) * _silu(z_ref[...].astype(F32))
    yn = yz * lax.rsqrt(jnp.mean(yz * yz, axis=-1, keepdims=True) + EPS) * sg_ref[...]
    out_b = _dot(yn.astype(BF16), wb_ref[...])
    merged = (jax.nn.sigmoid(ga_ref[...].astype(F32)) * out_a
              + jax.nn.sigmoid(gb_ref[...].astype(F32)) * out_b)
    gate1 = mod_ref[0, :, 2 * D_MODEL:3 * D_MODEL]
    out_ref[...] = x_ref[...] + gate1 * _dot(merged.astype(BF16), wo_ref[...])


def _merge(x2d, mod, o, y, p2d, ssm_g, wa, wb, wo, tm):
    m = x2d.shape[0]
    rows_per_mod = m // mod.shape[0]
    return pl.pallas_call(
        _merge_kernel,
        grid=(m // tm,),
        in_specs=[pl.BlockSpec((tm, D_MODEL), lambda i: (i, 0)),
                  pl.BlockSpec((1, 1, 6 * D_MODEL), lambda i: ((i * tm) // rows_per_mod, 0, 0)),
                  pl.BlockSpec((tm, D_MODEL), lambda i: (i, 0)),
                  pl.BlockSpec((tm, D_INNER), lambda i: (i, 0)),
                  pl.BlockSpec((tm, D_INNER), lambda i: (i, P_Z // D_INNER)),
                  pl.BlockSpec((tm, D_MODEL), lambda i: (i, P_GA // D_MODEL)),
                  pl.BlockSpec((tm, D_MODEL), lambda i: (i, P_GB // D_MODEL)),
                  _resident((1, D_INNER)),
                  _resident((D_MODEL, D_MODEL)),
                  _resident((D_INNER, D_MODEL)),
                  _resident((D_MODEL, D_MODEL))],
        out_specs=pl.BlockSpec((tm, D_MODEL), lambda i: (i, 0)),
        out_shape=jax.ShapeDtypeStruct((m, D_MODEL), F32),
        compiler_params=_cparams(("arbitrary",)),
        name="merge",
    )(x2d, mod, o, y, p2d, p2d, p2d, ssm_g, wa, wb, wo)


def _ffn_kernel(x_ref, mod_ref, g_ref, wg_ref, wu_ref, wd_ref, out_ref):
    x = x_ref[...]
    y = x * lax.rsqrt(jnp.mean(x * x, axis=-1, keepdims=True) + EPS) * g_ref[...]
    shift = mod_ref[0, :, 3 * D_MODEL:4 * D_MODEL]
    scale = mod_ref[0, :, 4 * D_MODEL:5 * D_MODEL]
    gate = mod_ref[0, :, 5 * D_MODEL:6 * D_MODEL]
    h = (y * (1.0 + scale) + shift).astype(BF16)
    f = (_silu(_dot(h, wg_ref[...])) * _dot(h, wu_ref[...])).astype(BF16)
    out_ref[...] = x + gate * _dot(f, wd_ref[...])


def _ffn(x2d, mod, g2, wg, wu, wd, tm):
    m = x2d.shape[0]
    rows_per_mod = m // mod.shape[0]
    return pl.pallas_call(
        _ffn_kernel,
        grid=(m // tm,),
        in_specs=[pl.BlockSpec((tm, D_MODEL), lambda i: (i, 0)),
                  pl.BlockSpec((1, 1, 6 * D_MODEL), lambda i: ((i * tm) // rows_per_mod, 0, 0)),
                  _resident((1, D_MODEL)),
                  _resident((D_MODEL, D_FF)),
                  _resident((D_MODEL, D_FF)),
                  _resident((D_FF, D_MODEL))],
        out_specs=pl.BlockSpec((tm, D_MODEL), lambda i: (i, 0)),
        out_shape=jax.ShapeDtypeStruct((m, D_MODEL), F32),
        compiler_params=_cparams(("arbitrary",)),
        name="ffn",
    )(x2d, mod, g2, wg, wu, wd)


def _rope_tables(n_tok):
    pos = jnp.arange(n_tok, dtype=jnp.int32)
    row = (pos // GRID_W).astype(F32)
    colp = (pos % GRID_W).astype(F32)
    inv = ROPE_BASE ** (-jnp.arange(0, ROPE_AXIS_DIM, 2, dtype=F32) / ROPE_AXIS_DIM)
    ang_r = row[:, None] * inv[None, :]
    ang_c = colp[:, None] * inv[None, :]
    zero = jnp.zeros_like(ang_r)
    cos = jnp.concatenate([jnp.cos(ang_r)] * 2 + [jnp.cos(ang_c)] * 2, axis=-1)
    sup = jnp.concatenate([-jnp.sin(ang_r), zero, -jnp.sin(ang_c), zero], axis=-1)
    sdn = jnp.concatenate([zero, jnp.sin(ang_r), zero, jnp.sin(ang_c)], axis=-1)
    return tuple(jnp.tile(t, (1, 2)) for t in (cos, sup, sdn))


def _pad_lanes(v):
    flat = v.reshape(1, -1).astype(F32)
    return jnp.pad(flat, ((0, 0), (0, LANES - flat.shape[1])))


def kernel(x_prompt, x_sample, c, cache_k, cache_v, state_ssm_fwd, state_ssm_bwd, c_ctx, norm1_g, norm2_g, w_ada, b_ada, w_in, q_norm_g, k_norm_g, lambda_q1, lambda_k1, lambda_q2, lambda_k2, attn_sub_g, conv_w, conv_b, A_log, dt_bias, D_skip, ssm_norm_g, w_branch_a, w_branch_b, w_out, w_ffn_gate, w_ffn_up, w_ffn_down):
    depth = norm1_g.shape[0]
    assert depth == 1, "single-layer kernel"
    l = 0
    lam_init = 0.8 - 0.6 * math.exp(-0.3 * l)
    nb, seq, _ = x_prompt.shape
    db, dseq, _ = x_sample.shape

    cond = jnp.concatenate([c_ctx[None, :], c, jnp.zeros((SUBLANES - 1 - db, D_MODEL), F32)], axis=0)
    mod = _ada(cond, w_ada[l], b_ada[l][None, :])
    mod_ctx = mod[0:1].reshape(1, 1, 6 * D_MODEL)
    mod_lat = mod[1:1 + db].reshape(db, 1, 6 * D_MODEL)

    w = jnp.transpose(w_in[l])
    w_gates = w[W_G0:W_G0 + 2 * D_MODEL]
    w_dt = jnp.pad(w[W_DT0:W_G0], ((0, LANES - 2 * SSM_HEADS), (0, 0)))
    wa = w_branch_a[l].astype(BF16)
    wb = w_branch_b[l].astype(BF16)
    wo = w_out[l].astype(BF16)
    wg = w_ffn_gate[l].astype(BF16)
    wu = w_ffn_up[l].astype(BF16)
    wd = w_ffn_down[l].astype(BF16)
    g1 = norm1_g[l][None, :]
    g2 = norm2_g[l][None, :]
    qg = jnp.tile(q_norm_g[l], 2)[None, :]
    kg = jnp.tile(k_norm_g[l], 2)[None, :]
    sg = attn_sub_g[l][None, :]
    lamp = jnp.stack([lambda_q1[l], lambda_k1[l], lambda_q2[l], lambda_k2[l]], axis=0)
    bias_row = _pad_lanes(dt_bias[l])
    alog_row = _pad_lanes(A_log[l])
    dexp = jnp.repeat(D_skip[l], SSM_HEADDIM)[None, :]
    cw = conv_w[l]
    cb = conv_b[l][None, :]
    ssm_g = ssm_norm_g[l][None, :]

    def layer(x, mod_rows, rope_tabs, ctx, h0, emit_kv):
        batch, sl, _ = x.shape
        x2d = x.reshape(batch * sl, D_MODEL)
        h, dt_raw = _norm1(x2d, mod_rows, g1, w_dt, tm=512)
        p2d = _proj(h, w, w_gates, tm=2048)
        cs, rtT = _dtprep(dt_raw, bias_row, alog_row)
        hps = 2 if ctx is not None else max(1, min(ATTN_HEADS, 1024 // sl))
        attn_out = _attn(p2d, lamp, qg, kg, sg, batch, sl, lam_init, hps, rope_tabs, ctx, emit_kv)
        y, hf, hb = _ssd(p2d, cs, rtT, cw, cb, dexp, batch, sl, SSD_GROUPS_PER_STEP, h0)
        x1 = _merge(x2d, mod_rows, attn_out[0], y, p2d, ssm_g, wa, wb, wo, tm=512)
        x2 = _ffn(x1, mod_rows, g2, wg, wu, wd, tm=512)
        return x2.reshape(batch, sl, D_MODEL), attn_out[1:], hf, hb

    y_prompt, kv, hf, hb = layer(x_prompt, mod_ctx, None, None, None, True)
    ctx_kT = jnp.transpose(cache_k[:, l], (0, 2, 3, 4, 1)).reshape(db, ATTN_HEADS, HEAD_W, -1)
    ctx_v = cache_v[:, l].reshape(db, -1, ATTN_VD)
    h0 = (state_ssm_fwd[:, l].reshape(db, D_INNER, SSM_STATE),
          state_ssm_bwd[:, l].reshape(db, D_INNER, SSM_STATE))
    y_sample, _, _, _ = layer(x_sample, mod_lat, _rope_tables(dseq), (ctx_kT, ctx_v), h0, False)

    knT = kv[0].reshape(nb, ATTN_HEADS, 2, ATTN_DH, seq)
    new_cache_k = jnp.transpose(knT, (0, 4, 1, 2, 3)).reshape(nb, 1, seq, ATTN_HEADS, 2, ATTN_DH)
    new_cache_v = kv[1].reshape(nb, 1, seq, ATTN_HEADS, ATTN_VD)
    new_hf = hf.reshape(nb, 1, SSM_HEADS, SSM_HEADDIM, SSM_STATE)
    new_hb = hb.reshape(nb, 1, SSM_HEADS, SSM_HEADDIM, SSM_STATE)
    return (y_prompt, y_sample, new_cache_k, new_cache_v, new_hf, new_hb)
```

```python
import functools
import math

import jax
import jax.numpy as jnp
from jax import lax
from jax.experimental import pallas as pl
from jax.experimental.pallas import tpu as pltpu

F32 = jnp.float32
BF16 = jnp.bfloat16

D_MODEL = 1024
GRID_W = 64
ATTN_HEADS = 8
ATTN_DH = 64
ATTN_VD = 128
HEAD_W = 2 * ATTN_DH
ROPE_BASE = 10000.0
ROPE_AXIS_DIM = ATTN_DH // 2
ROPE_HALF = ROPE_AXIS_DIM // 2
D_INNER = 2 * D_MODEL
SSM_HEADDIM = 64
SSM_HEADS = D_INNER // SSM_HEADDIM
SSM_GROUPS = 8
HEADS_PER_GROUP = SSM_HEADS // SSM_GROUPS
GROUP_W = HEADS_PER_GROUP * SSM_HEADDIM
SSM_STATE = 128
D_CONV = 5
CONV_PAD = 8
CONV_EDGE = 16
CHUNK = 128
D_FF = ((8 * D_MODEL // 3 + 255) // 256) * 256
EPS = 1e-6
LANES = 128
SUBLANES = 8
LOG2E = math.log2(math.e)

P_Z = 0
P_GA = P_Z + D_INNER
P_GB = P_GA + D_MODEL
P_X = P_GB + D_MODEL
P_B = P_X + D_INNER
P_C = P_B + SSM_GROUPS * SSM_STATE
P_Q = P_C + SSM_GROUPS * SSM_STATE
P_K = P_Q + D_MODEL
P_V = P_K + D_MODEL
NP = P_V + D_MODEL
PROJ_TN = 1024
ACT_DTYPE = BF16

W_Q0 = 0
W_Z0 = 3 * D_MODEL
W_XBC0 = W_Z0 + D_INNER
W_DT0 = W_XBC0 + D_INNER + 2 * SSM_GROUPS * SSM_STATE
W_G0 = W_DT0 + 2 * SSM_HEADS

VMEM_LIMIT = 56 * 1024 * 1024


def _cparams(sem):
    return pltpu.CompilerParams(dimension_semantics=sem, vmem_limit_bytes=VMEM_LIMIT)


def _silu(x):
    return x * jax.nn.sigmoid(x)


def _dot(a, b):
    return jnp.dot(a, b, preferred_element_type=F32)


def _dot_nt(a, b):
    return lax.dot_general(a, b, (((1,), (1,)), ((), ())), preferred_element_type=F32)


def _split3(x):
    hi = x.astype(BF16)
    r = x - hi.astype(F32)
    mid = r.astype(BF16)
    lo = (r - mid.astype(F32)).astype(BF16)
    return hi, mid, lo


def _resident(shape):
    return pl.BlockSpec(shape, lambda *_: (0,) * len(shape), pipeline_mode=pl.Buffered(1))


def _ada_kernel(cond_ref, w_ref, b_ref, o_ref):
    s = _silu(cond_ref[...]).astype(BF16)
    o_ref[...] = _dot(s, w_ref[...].astype(BF16)) + b_ref[...]


def _ada(cond, w_ada, b_ada):
    rows = cond.shape[0]
    n = w_ada.shape[1]
    tn = 1024
    return pl.pallas_call(
        _ada_kernel,
        grid=(n // tn,),
        in_specs=[pl.BlockSpec((rows, D_MODEL), lambda j: (0, 0)),
                  pl.BlockSpec((D_MODEL, tn), lambda j: (0, j)),
                  pl.BlockSpec((1, tn), lambda j: (0, j))],
        out_specs=pl.BlockSpec((rows, tn), lambda j: (0, j)),
        out_shape=jax.ShapeDtypeStruct((rows, n), F32),
        compiler_params=_cparams(("arbitrary",)),
        name="ada",
    )(cond, w_ada, b_ada)


def _norm1_kernel(x_ref, mod_ref, g_ref, wdt_ref, h_ref, dt_ref):
    x = x_ref[...]
    y = x * lax.rsqrt(jnp.mean(x * x, axis=-1, keepdims=True) + EPS) * g_ref[...]
    shift = mod_ref[0, :, 0:D_MODEL]
    scale = mod_ref[0, :, D_MODEL:2 * D_MODEL]
    h = (y * (1.0 + scale) + shift).astype(BF16)
    h_ref[...] = h
    dt_ref[...] = _dot_nt(h, wdt_ref[...].astype(BF16))


def _norm1(x2d, mod, g1, w_dt, tm):
    m = x2d.shape[0]
    rows_per_mod = m // mod.shape[0]
    return pl.pallas_call(
        _norm1_kernel,
        grid=(m // tm,),
        in_specs=[pl.BlockSpec((tm, D_MODEL), lambda i: (i, 0)),
                  pl.BlockSpec((1, 1, 6 * D_MODEL), lambda i: ((i * tm) // rows_per_mod, 0, 0)),
                  _resident((1, D_MODEL)),
                  _resident((LANES, D_MODEL))],
        out_specs=[pl.BlockSpec((tm, D_MODEL), lambda i: (i, 0)),
                   pl.BlockSpec((tm, LANES), lambda i: (i, 0))],
        out_shape=[jax.ShapeDtypeStruct((m, D_MODEL), BF16),
                   jax.ShapeDtypeStruct((m, LANES), F32)],
        compiler_params=_cparams(("arbitrary",)),
        name="norm1",
    )(x2d, mod, g1, w_dt)


_GATE_TILE0 = P_GA // PROJ_TN
_N_GATE_TILES = 2 * D_MODEL // PROJ_TN


def _w_in_tile(j):
    z_tiles = D_INNER // PROJ_TN
    xbc_tile0 = (P_X // PROJ_TN)
    qkv_tile0 = (P_Q // PROJ_TN)
    z_src = jnp.minimum(j, z_tiles - 1) + W_Z0 // PROJ_TN
    xbc_src = j - xbc_tile0 + W_XBC0 // PROJ_TN
    qkv_src = j - qkv_tile0 + W_Q0 // PROJ_TN
    return jnp.where(j < xbc_tile0, z_src, jnp.where(j < qkv_tile0, xbc_src, qkv_src))


def _proj_kernel(h_ref, win_ref, wg_ref, p_ref, w_scr, *, tm):
    j = pl.program_id(0)
    i = pl.program_id(1)
    is_gate = jnp.logical_and(j >= _GATE_TILE0, j < _GATE_TILE0 + _N_GATE_TILES)

    @pl.when(jnp.logical_and(i == 0, is_gate))
    def _():
        w_scr[...] = wg_ref[...].astype(BF16)

    @pl.when(jnp.logical_and(i == 0, jnp.logical_not(is_gate)))
    def _():
        w_scr[...] = win_ref[...].astype(BF16)

    r0 = pl.multiple_of(i * tm, tm)
    p_ref[...] = _dot_nt(h_ref[pl.ds(r0, tm), :], w_scr[...]).astype(p_ref.dtype)


def _proj(h, w_in_t, w_gates_t, tm):
    m = h.shape[0]
    tn = PROJ_TN
    return pl.pallas_call(
        functools.partial(_proj_kernel, tm=tm),
        grid=(NP // tn, m // tm),
        in_specs=[_resident((m, D_MODEL)),
                  pl.BlockSpec((tn, D_MODEL), lambda j, i: (_w_in_tile(j), 0)),
                  pl.BlockSpec((tn, D_MODEL),
                               lambda j, i: (jnp.clip(j - _GATE_TILE0, 0, _N_GATE_TILES - 1), 0))],
        out_specs=pl.BlockSpec((tm, tn), lambda j, i: (i, j)),
        out_shape=jax.ShapeDtypeStruct((m, NP), ACT_DTYPE),
        scratch_shapes=[pltpu.VMEM((tn, D_MODEL), BF16)],
        compiler_params=_cparams(("arbitrary", "arbitrary")),
        name="proj",
    )(h, w_in_t, w_gates_t)


DTPREP_CHUNKS = 4
SSD_GROUPS_PER_STEP = 2

def _dtprep_kernel(dtr_ref, bias_ref, alog_ref, cs_ref, rtT_ref):
    row = lax.broadcasted_iota(jnp.int32, (CHUNK, CHUNK), 0)
    col = lax.broadcasted_iota(jnp.int32, (CHUNK, CHUNK), 1)
    tril = jnp.where(col <= row, 1.0, 0.0).astype(BF16)
    triu = jnp.where(col >= row, 1.0, 0.0).astype(BF16)
    neg_a = -jnp.exp(alog_ref[...])
    for c in range(DTPREP_CHUNKS):
        sl = slice(c * CHUNK, (c + 1) * CHUNK)
        x = dtr_ref[sl, :] + bias_ref[...]
        dt = jnp.maximum(x, 0.0) + jnp.log1p(jnp.exp(-jnp.abs(x)))
        hi, mid, lo = _split3(dt * neg_a)
        cf = _dot(tril, hi) + _dot(tril, mid) + _dot(tril, lo)
        cb = _dot(triu, hi) + _dot(triu, mid) + _dot(triu, lo)
        cs = jnp.where(col < SSM_HEADS, cf, cb)
        cs_ref[sl, :] = cs * LOG2E
        rtT_ref[c] = ((cs - jnp.log(dt)) * LOG2E).T


def _dtprep(dt_raw, bias_row, alog_row):
    m = dt_raw.shape[0]
    rows = DTPREP_CHUNKS * CHUNK
    return pl.pallas_call(
        _dtprep_kernel,
        grid=(m // rows,),
        in_specs=[pl.BlockSpec((rows, LANES), lambda i: (i, 0)),
                  _resident((1, LANES)),
                  _resident((1, LANES))],
        out_specs=[pl.BlockSpec((rows, LANES), lambda i: (i, 0)),
                   pl.BlockSpec((DTPREP_CHUNKS, CHUNK, CHUNK), lambda i: (i, 0, 0))],
        out_shape=[jax.ShapeDtypeStruct((m, LANES), F32),
                   jax.ShapeDtypeStruct((m // CHUNK, CHUNK, CHUNK), F32)],
        compiler_params=_cparams(("arbitrary",)),
        name="dtprep",
    )(dt_raw, bias_row, alog_row)


def _half_norm(x, g, half_ones):
    ss = x * x
    hi = ss.astype(BF16)
    lo = (ss - hi.astype(F32)).astype(BF16)
    ms = (_dot(hi, half_ones) + _dot(lo, half_ones)) * (1.0 / ATTN_DH)
    return x * lax.rsqrt(ms + EPS) * g


def _rope(x, cos, sin_up, sin_dn):
    x_up = pltpu.roll(x, HEAD_W - ROPE_HALF, axis=1)
    x_dn = pltpu.roll(x, ROPE_HALF, axis=1)
    return x * cos + x_up * sin_up + x_dn * sin_dn


def _attn_kernel(*refs, lq, lctx, hps, lam_init, rope, emit_kv):
    it = iter(refs)
    lamp_ref, qg_ref, kg_ref, sg_ref, q_ref, k_ref, v_ref = (next(it) for _ in range(7))
    if rope:
        cos_ref, sup_ref, sdn_ref = next(it), next(it), next(it)
    if lctx:
        ckT_ref, cv_ref = next(it), next(it)
    o_ref = next(it)
    if emit_kv:
        knT_ref, vo_ref = next(it), next(it)
    head_scr = [tuple(next(it) for _ in range(4)) for _ in range(hps)]
    if lctx:
        kcT_scrs = [next(it) for _ in range(hps)]

    lp = lamp_ref[...]
    lam = (jnp.exp(jnp.sum(lp[0:1] * lp[1:2], axis=-1, keepdims=True))
           - jnp.exp(jnp.sum(lp[2:3] * lp[3:4], axis=-1, keepdims=True)) + lam_init)

    r_i = lax.broadcasted_iota(jnp.int32, (HEAD_W, HEAD_W), 0) // ATTN_DH
    c_i = lax.broadcasted_iota(jnp.int32, (HEAD_W, HEAD_W), 1) // ATTN_DH
    half_ones = jnp.where(r_i == c_i, 1.0, 0.0).astype(BF16)

    first_half = lax.broadcasted_iota(jnp.int32, (lq, HEAD_W), 1) < ATTN_DH
    sub_gain = sg_ref[...] * (1.0 - lam_init)
    ones = jnp.ones((lq + lctx, ATTN_VD), BF16)

    def stage_ctx_values(va_scr, hh):
        head = pl.program_id(1) * hps + hh
        rows = cv_ref[0, pl.ds(head, lctx, stride=ATTN_HEADS), :]
        va_scr[lq:lq + lctx, 0:ATTN_VD] = rows.astype(BF16)

    for hh in range(hps):
        cols = slice(hh * HEAD_W, (hh + 1) * HEAD_W)
        qn = _half_norm(q_ref[:, cols].astype(F32), qg_ref[...], half_ones)
        kn = _half_norm(k_ref[:, cols].astype(F32), kg_ref[...], half_ones)
        v = v_ref[:, cols]
        if emit_kv:
            knT_ref[0, hh] = kn.T
            vo_ref[:, cols] = v.astype(F32)
        if rope:
            qn = _rope(qn, cos_ref[...], sup_ref[...], sdn_ref[...])
            kn = _rope(kn, cos_ref[...], sup_ref[...], sdn_ref[...])
        qs = qn * (ATTN_DH ** -0.5 * LOG2E)
        q1_scr, q2_scr, kk_scr, va_scr = head_scr[hh]
        q1_scr[...] = jnp.where(first_half, qs, 0.0).astype(BF16)
        q2_scr[...] = jnp.where(first_half, 0.0, qs).astype(BF16)
        kk_scr[...] = kn.astype(BF16)
        va_scr[0:lq, 0:ATTN_VD] = v.astype(BF16)
        va_scr[:, ATTN_VD:2 * ATTN_VD] = ones
        if lctx:
            kcT_scrs[hh][...] = ckT_ref[0, hh].astype(BF16)
            stage_ctx_values(va_scr, hh)

        def softmax_pv(qb, kk_scr=kk_scr, va_scr=va_scr, hh=hh):
            s = _dot_nt(qb, kk_scr[...])
            if lctx:
                s = jnp.concatenate([s, _dot(qb, kcT_scrs[hh][...])], axis=1)
            p = jnp.exp2(s - jnp.max(s, axis=-1, keepdims=True)).astype(BF16)
            a = _dot(p, va_scr[...])
            return a[:, 0:ATTN_VD] / a[:, ATTN_VD:2 * ATTN_VD]

        o = softmax_pv(q1_scr[...]) - lam * softmax_pv(q2_scr[...])
        o = o * lax.rsqrt(jnp.mean(o * o, axis=-1, keepdims=True) + EPS) * sub_gain
        o_ref[:, cols] = o.astype(o_ref.dtype)


def _attn(p2d, lamp, qg, kg, sg, batch, lq, lam_init, hps, rope_tabs=None, ctx=None, emit_kv=False):
    m = p2d.shape[0]
    lctx = 0 if ctx is None else ctx[0].shape[-1]
    width = hps * HEAD_W
    qb, kb, vb = P_Q // width, P_K // width, P_V // width
    const = lambda b, h: (0, 0)
    in_specs = [pl.BlockSpec((4, ATTN_DH), const),
                pl.BlockSpec((1, HEAD_W), const),
                pl.BlockSpec((1, HEAD_W), const),
                pl.BlockSpec((1, ATTN_VD), const),
                pl.BlockSpec((lq, width), lambda b, h: (b, qb + h)),
                pl.BlockSpec((lq, width), lambda b, h: (b, kb + h)),
                pl.BlockSpec((lq, width), lambda b, h: (b, vb + h))]
    args = [lamp, qg, kg, sg, p2d, p2d, p2d]
    if rope_tabs is not None:
        in_specs += [pl.BlockSpec((lq, HEAD_W), const)] * 3
        args += list(rope_tabs)
    if ctx is not None:
        in_specs += [pl.BlockSpec((1, hps, HEAD_W, lctx), lambda b, h: (b, h, 0, 0)),
                     pl.BlockSpec((1, lctx * ATTN_HEADS, ATTN_VD), lambda b, h: (b, 0, 0))]
        args += list(ctx)
    head_out = pl.BlockSpec((lq, width), lambda b, h: (b, h))
    out_specs = [head_out]
    out_shape = [jax.ShapeDtypeStruct((m, D_MODEL), ACT_DTYPE)]
    if emit_kv:
        out_specs += [pl.BlockSpec((1, hps, HEAD_W, lq), lambda b, h: (b, h, 0, 0)), head_out]
        out_shape += [jax.ShapeDtypeStruct((batch, ATTN_HEADS, HEAD_W, lq), F32),
                      jax.ShapeDtypeStruct((m, D_MODEL), F32)]
    scratch = hps * [pltpu.VMEM((lq, HEAD_W), BF16),
                     pltpu.VMEM((lq, HEAD_W), BF16),
                     pltpu.VMEM((lq, HEAD_W), BF16),
                     pltpu.VMEM((lq + lctx, 2 * ATTN_VD), BF16)]
    if lctx:
        scratch += hps * [pltpu.VMEM((HEAD_W, lctx), BF16)]
    kern = functools.partial(_attn_kernel, lq=lq, lctx=lctx, hps=hps, lam_init=lam_init,
                             rope=rope_tabs is not None, emit_kv=emit_kv)
    return pl.pallas_call(
        kern,
        grid=(batch, ATTN_HEADS // hps),
        in_specs=in_specs,
        out_specs=out_specs,
        out_shape=out_shape,
        scratch_shapes=scratch,
        compiler_params=_cparams(("arbitrary", "arbitrary")),
        name="attn",
    )(*args)


def _ssd_kernel(*refs, seq, has_h0, gps):
    it = iter(refs)
    x_ref, b_ref, c_ref = next(it), next(it), next(it)
    cwx_ref, cwb_ref, cwc_ref = next(it), next(it), next(it)
    cbx_ref, cbb_ref, cbc_ref = next(it), next(it), next(it)
    dexp_ref, cs_ref, rtT_ref = next(it), next(it), next(it)
    if has_h0:
        h0f_ref, h0b_ref = next(it), next(it)
    y_ref, hf_ref, hb_ref = next(it), next(it), next(it)
    group_scr = [tuple(next(it) for _ in range(10)) for _ in range(gps)]

    g0 = pl.program_id(1) * gps
    nc = seq // CHUNK
    x_lanes = [slice(gi * GROUP_W, (gi + 1) * GROUP_W) for gi in range(gps)]
    n_lanes = [slice(gi * SSM_STATE, (gi + 1) * SSM_STATE) for gi in range(gps)]
    blk = lax.broadcasted_iota(jnp.int32, (CHUNK, GROUP_W), 1) // SSM_HEADDIM

    win_rows = CHUNK + 2 * CONV_EDGE
    sel_r = lax.broadcasted_iota(jnp.int32, (CHUNK, win_rows), 0)
    sel_c = lax.broadcasted_iota(jnp.int32, (CHUNK, win_rows), 1)
    taps = [d for d in range(-(D_CONV // 2), D_CONV // 2 + 1) if d != 0]
    shift_sel = {d: jnp.where(sel_c == sel_r + CONV_EDGE + d, 1.0, 0.0).astype(BF16) for d in taps}

    def conv_chunk(c, sources, shift_on_mxu):
        r0 = pl.multiple_of(c * CHUNK, CHUNK)
        p0 = pl.multiple_of(jnp.maximum(r0 - CONV_EDGE, 0), CONV_EDGE)
        n0 = pl.multiple_of(jnp.minimum(r0 + CHUNK, seq - CONV_EDGE), CONV_EDGE)

        def gather(rows_of):
            parts = [rows_of(*src) for src in sources]
            return parts[0] if len(parts) == 1 else jnp.concatenate(parts, axis=1)

        assert all(src[0].dtype == BF16 for src in sources)
        prev = gather(lambda ref, w_ref, b_ref, lanes: ref[pl.ds(p0, CONV_EDGE), lanes])
        nxt = gather(lambda ref, w_ref, b_ref, lanes: ref[pl.ds(n0, CONV_EDGE), lanes])
        main = gather(lambda ref, w_ref, b_ref, lanes: ref[pl.ds(r0, CHUNK), lanes])
        w = [gather(lambda ref, w_ref, b_ref, lanes, j=j: w_ref[j:j + 1, lanes]) for j in range(D_CONV)]
        bias = gather(lambda ref, w_ref, b_ref, lanes: b_ref[:, lanes])
        acc = bias + main.astype(F32) * w[D_CONV // 2]
        if shift_on_mxu:
            zero = jnp.zeros_like(prev)
            win = jnp.concatenate([jnp.where(c > 0, prev, zero), main,
                                   jnp.where(c < nc - 1, nxt, zero)], axis=0)
            for d in taps:
                acc = acc + _dot(shift_sel[d], win) * w[d + D_CONV // 2]
        else:
            rows = CHUNK + 2 * CONV_PAD
            win = jnp.concatenate(
                [jnp.where(c > 0, prev.astype(F32)[CONV_EDGE - CONV_PAD:, :], 0.0), main.astype(F32),
                 jnp.where(c < nc - 1, nxt.astype(F32)[:CONV_PAD, :], 0.0)], axis=0)
            for d in taps:
                tap = pltpu.roll(win, (-d) % rows, axis=0)[CONV_PAD:CONV_PAD + CHUNK, :]
                acc = acc + tap * w[d + D_CONV // 2]
        return _silu(acc)

    head_masks = [jnp.where(blk == j, 1.0, 0.0).astype(BF16) for j in range(HEADS_PER_GROUP)]

    def conv_body(c, carry):
        sl = pl.ds(pl.multiple_of(c * CHUNK, CHUNK), CHUNK)
        for gi in range(gps):
            xm_s, bT_s, cc_s, _, _, _, yacc, g_s, _, _ = group_scr[gi]
            xv = conv_chunk(c, [(x_ref, cwx_ref, cbx_ref, x_lanes[gi])], nc > 2)
            yacc[sl, :] = dexp_ref[:, x_lanes[gi]] * xv
            xb = xv.astype(BF16)
            for j in range(HEADS_PER_GROUP):
                xm_s[c, j * CHUNK:(j + 1) * CHUNK, :] = xb * head_masks[j]
            bc = conv_chunk(c, [(b_ref, cwb_ref, cbb_ref, n_lanes[gi]),
                                (c_ref, cwc_ref, cbc_ref, n_lanes[gi])], False)
            bT = bc[:, 0:SSM_STATE].T.astype(BF16)
            cc = bc[:, SSM_STATE:2 * SSM_STATE].astype(BF16)
            bT_s[c] = bT
            cc_s[sl, :] = cc
            g_s[c] = _dot(cc, bT)
        return carry

    lax.fori_loop(0, nc, conv_body, 0, unroll=2)

    for gi in range(gps):
        csg_s, stf, stb = group_scr[gi][3:6]
        shift = (LANES - HEADS_PER_GROUP * (g0 + gi)) % LANES
        csg_s[...] = pltpu.roll(cs_ref[...], shift, axis=1)
        if has_h0:
            stf[...] = h0f_ref[0, x_lanes[gi], :].T
            stb[...] = h0b_ref[0, x_lanes[gi], :].T
        else:
            stf[...] = jnp.zeros_like(stf)
            stb[...] = jnp.zeros_like(stb)

    row = lax.broadcasted_iota(jnp.int32, (CHUNK, CHUNK), 0)
    col = lax.broadcasted_iota(jnp.int32, (CHUNK, CHUNK), 1)
    lane_lo = lax.broadcasted_iota(jnp.int32, (CHUNK, LANES), 1) < SSM_HEADDIM

    def decay_terms(c, gi):
        xm_s, bT_s, cc_s, csg_s, _, _, yacc, g_s, upd_s, e_s = group_scr[gi]
        g = g0 + gi
        sl = pl.ds(pl.multiple_of(c * CHUNK, CHUNK), CHUNK)
        bT = bT_s[c]
        cs = csg_s[sl, :]
        x_heads = xm_s[c]
        dec_sum = [None] * HEADS_PER_GROUP
        for d, (lane0, mask, edge) in enumerate(((0, col <= row, CHUNK - 1), (SSM_HEADS, col >= row, 0))):
            upd_lhs, cols = [], []
            for j in range(HEADS_PER_GROUP):
                cb = jnp.broadcast_to(cs[:, lane0 + j:lane0 + j + 1], (CHUNK, CHUNK))
                rrow = rtT_ref[c, pl.ds(lane0 + HEADS_PER_GROUP * g + j, 1), :]
                dec = jnp.exp2(jnp.where(mask, cb - rrow, -jnp.inf))
                dec_sum[j] = dec if dec_sum[j] is None else dec_sum[j] + dec
                w_edge = dec[edge:edge + 1, :]
                upd_lhs.append(bT * w_edge.astype(BF16))
                cols.append(cb)
            upd_s[d, c] = _dot(jnp.concatenate(upd_lhs, axis=1), x_heads)
            cs_e = jnp.concatenate([jnp.where(lane_lo, cols[2 * t], cols[2 * t + 1])
                                    for t in range(HEADS_PER_GROUP // 2)], axis=1)
            e_s[d, c] = jnp.exp2(cs_e)
        gmat = g_s[c]
        diag_lhs = [(gmat * dec_sum[j]).astype(BF16) for j in range(HEADS_PER_GROUP)]
        yacc[sl, :] = yacc[sl, :] + _dot(jnp.concatenate(diag_lhs, axis=1), x_heads)

    def decay_body(c, carry):
        for gi in range(gps):
            decay_terms(c, gi)
        return carry

    lax.fori_loop(0, nc, decay_body, 0, unroll=2)

    def state_step(c, d, gi):
        _, _, cc_s, _, stf, stb, yacc, _, upd_s, e_s = group_scr[gi]
        st_ref = stb if d else stf
        edge = 0 if d else CHUNK - 1
        sl = pl.ds(pl.multiple_of(c * CHUNK, CHUNK), CHUNK)
        e = e_s[d, c]
        st = st_ref[...]
        yacc[sl, :] = yacc[sl, :] + e * _dot(cc_s[sl, :], st.astype(BF16))
        st_ref[...] = e[edge:edge + 1, :] * st + upd_s[d, c]

    def state_body(i, carry):
        for gi in range(gps):
            state_step(i, 0, gi)
            state_step(nc - 1 - i, 1, gi)
        return carry

    lax.fori_loop(0, nc, state_body, 0, unroll=2)

    for gi in range(gps):
        stf, stb, yacc = group_scr[gi][4:7]
        y_ref[:, x_lanes[gi]] = yacc[...].astype(y_ref.dtype)
        hf_ref[0, x_lanes[gi], :] = stf[...].T
        hb_ref[0, x_lanes[gi], :] = stb[...].T


def _ssd(p2d, cs, rtT, conv_w, conv_b, dexp, batch, seq, gps, h0=None):
    m = p2d.shape[0]
    nc = seq // CHUNK
    xw, nw = gps * GROUP_W, gps * SSM_STATE
    xb, bb, cb = P_X // xw, P_B // nw, P_C // nw
    cw_b0 = D_INNER // nw
    cw_c0 = cw_b0 + SSM_GROUPS // gps
    in_specs = [pl.BlockSpec((seq, xw), lambda b, g: (b, xb + g)),
                pl.BlockSpec((seq, nw), lambda b, g: (b, bb + g)),
                pl.BlockSpec((seq, nw), lambda b, g: (b, cb + g)),
                pl.BlockSpec((D_CONV, xw), lambda b, g: (0, g)),
                pl.BlockSpec((D_CONV, nw), lambda b, g: (0, cw_b0 + g)),
                pl.BlockSpec((D_CONV, nw), lambda b, g: (0, cw_c0 + g)),
                pl.BlockSpec((1, xw), lambda b, g: (0, g)),
                pl.BlockSpec((1, nw), lambda b, g: (0, cw_b0 + g)),
                pl.BlockSpec((1, nw), lambda b, g: (0, cw_c0 + g)),
                pl.BlockSpec((1, xw), lambda b, g: (0, g)),
                pl.BlockSpec((seq, LANES), lambda b, g: (b, 0)),
                pl.BlockSpec((nc, CHUNK, CHUNK), lambda b, g: (b, 0, 0))]
    args = [p2d, p2d, p2d, conv_w, conv_w, conv_w, conv_b, conv_b, conv_b, dexp, cs, rtT]
    state_spec = pl.BlockSpec((1, xw, SSM_STATE), lambda b, g: (b, g, 0))
    if h0 is not None:
        in_specs += [state_spec, state_spec]
        args += list(h0)
    state_shape = jax.ShapeDtypeStruct((batch, D_INNER, SSM_STATE), F32)
    kern = functools.partial(_ssd_kernel, seq=seq, has_h0=h0 is not None, gps=gps)
    return pl.pallas_call(
        kern,
        grid=(batch, SSM_GROUPS // gps),
        in_specs=in_specs,
        out_specs=[pl.BlockSpec((seq, xw), lambda b, g: (b, g)), state_spec, state_spec],
        out_shape=[jax.ShapeDtypeStruct((m, D_INNER), ACT_DTYPE), state_shape, state_shape],
        scratch_shapes=gps * [pltpu.VMEM((nc, HEADS_PER_GROUP * CHUNK, GROUP_W), BF16),
                              pltpu.VMEM((nc, SSM_STATE, CHUNK), BF16),
                              pltpu.VMEM((seq, SSM_STATE), BF16),
                              pltpu.VMEM((seq, LANES), F32),
                              pltpu.VMEM((SSM_STATE, GROUP_W), F32),
                              pltpu.VMEM((SSM_STATE, GROUP_W), F32),
                              pltpu.VMEM((seq, GROUP_W), F32),
                              pltpu.VMEM((nc, CHUNK, CHUNK), F32),
                              pltpu.VMEM((2, nc, SSM_STATE, GROUP_W), F32),
                              pltpu.VMEM((2, nc, CHUNK, GROUP_W), F32)],
        compiler_params=_cparams(("arbitrary", "arbitrary")),
        name="ssd",
    )(*args)


def _merge_kernel(x_ref, mod_ref, o_ref, y_ref, z_ref, ga_ref, gb_ref, sg_ref,
                  wa_ref, wb_ref, wo_ref, out_ref):
    out_a = _dot(o_ref[...], wa_ref[...])
    yz = y_ref[...].astype(F32) * _silu(z_ref[...].astype(F32))
    yn = yz * lax.rsqrt(jnp.mean(yz * yz, axis=-1, keepdims=True) + EPS) * sg_ref[...]
    out_b = _dot(yn.astype(BF16), wb_ref[...])
    merged = (jax.nn.sigmoid(ga_ref[...].astype(F32)) * out_a
              + jax.nn.sigmoid(gb_ref[...].astype(F32)) * out_b)
    gate1 = mod_ref[0, :, 2 * D_MODEL:3 * D_MODEL]
    out_ref[...] = x_ref[...] + gate1 * _dot(merged.astype(BF16), wo_ref[...])


def _merge(x2d, mod, o, y, p2d, ssm_g, wa, wb, wo, tm):
    m = x2d.shape[0]
    rows_per_mod = m // mod.shape[0]
    return pl.pallas_call(
        _merge_kernel,
        grid=(m // tm,),
        in_specs=[pl.BlockSpec((tm, D_MODEL), lambda i: (i, 0)),
                  pl.BlockSpec((1, 1, 6 * D_MODEL), lambda i: ((i * tm) // rows_per_mod, 0, 0)),
                  pl.BlockSpec((tm, D_MODEL), lambda i: (i, 0)),
                  pl.BlockSpec((tm, D_INNER), lambda i: (i, 0)),
                  pl.BlockSpec((tm, D_INNER), lambda i: (i, P_Z // D_INNER)),
                  pl.BlockSpec((tm, D_MODEL), lambda i: (i, P_GA // D_MODEL)),
                  pl.BlockSpec((tm, D_MODEL), lambda i: (i, P_GB // D_MODEL)),
                  _resident((1, D_INNER)),
                  _resident((D_MODEL, D_MODEL)),
                  _resident((D_INNER, D_MODEL)),
                  _resident((D_MODEL, D_MODEL))],
        out_specs=pl.BlockSpec((tm, D_MODEL), lambda i: (i, 0)),
        out_shape=jax.ShapeDtypeStruct((m, D_MODEL), F32),
        compiler_params=_cparams(("arbitrary",)),
        name="merge",
    )(x2d, mod, o, y, p2d, p2d, p2d, ssm_g, wa, wb, wo)


def _ffn_kernel(x_ref, mod_ref, g_ref, wg_ref, wu_ref, wd_ref, out_ref):
    x = x_ref[...]
    y = x * lax.rsqrt(jnp.mean(x * x, axis=-1, keepdims=True) + EPS) * g_ref[...]
    shift = mod_ref[0, :, 3 * D_MODEL:4 * D_MODEL]
    scale = mod_ref[0, :, 4 * D_MODEL:5 * D_MODEL]
    gate = mod_ref[0, :, 5 * D_MODEL:6 * D_MODEL]
    h = (y * (1.0 + scale) + shift).astype(BF16)
    f = (_silu(_dot(h, wg_ref[...])) * _dot(h, wu_ref[...])).astype(BF16)
    out_ref[...] = x + gate * _dot(f, wd_ref[...])


def _ffn(x2d, mod, g2, wg, wu, wd, tm):
    m = x2d.shape[0]
    rows_per_mod = m // mod.shape[0]
    return pl.pallas_call(
        _ffn_kernel,
        grid=(m // tm,),
        in_specs=[pl.BlockSpec((tm, D_MODEL), lambda i: (i, 0)),
                  pl.BlockSpec((1, 1, 6 * D_MODEL), lambda i: ((i * tm) // rows_per_mod, 0, 0)),
                  _resident((1, D_MODEL)),
                  _resident((D_MODEL, D_FF)),
                  _resident((D_MODEL, D_FF)),
                  _resident((D_FF, D_MODEL))],
        out_specs=pl.BlockSpec((tm, D_MODEL), lambda i: (i, 0)),
        out_shape=jax.ShapeDtypeStruct((m, D_MODEL), F32),
        compiler_params=_cparams(("arbitrary",)),
        name="ffn",
    )(x2d, mod, g2, wg, wu, wd)


def _rope_tables(n_tok):
    pos = jnp.arange(n_tok, dtype=jnp.int32)
    row = (pos // GRID_W).astype(F32)
    colp = (pos % GRID_W).astype(F32)
    inv = ROPE_BASE ** (-jnp.arange(0, ROPE_AXIS_DIM, 2, dtype=F32) / ROPE_AXIS_DIM)
    ang_r = row[:, None] * inv[None, :]
    ang_c = colp[:, None] * inv[None, :]
    zero = jnp.zeros_like(ang_r)
    cos = jnp.concatenate([jnp.cos(ang_r)] * 2 + [jnp.cos(ang_c)] * 2, axis=-1)
    sup = jnp.concatenate([-jnp.sin(ang_r), zero, -jnp.sin(ang_c), zero], axis=-1)
    sdn = jnp.concatenate([zero, jnp.sin(ang_r), zero, jnp.sin(ang_c)], axis=-1)
    return tuple(jnp.tile(t, (1, 2)) for t in (cos, sup, sdn))


def _pad_lanes(v):
    flat = v.reshape(1, -1).astype(F32)
    return jnp.pad(flat, ((0, 0), (0, LANES - flat.shape[1])))


def kernel(x_prompt, x_sample, c, cache_k, cache_v, state_ssm_fwd, state_ssm_bwd, c_ctx, norm1_g, norm2_g, w_ada, b_ada, w_in, q_norm_g, k_norm_g, lambda_q1, lambda_k1, lambda_q2, lambda_k2, attn_sub_g, conv_w, conv_b, A_log, dt_bias, D_skip, ssm_norm_g, w_branch_a, w_branch_b, w_out, w_ffn_gate, w_ffn_up, w_ffn_down):
    depth = norm1_g.shape[0]
    assert depth == 1, "single-layer kernel"
    l = 0
    lam_init = 0.8 - 0.6 * math.exp(-0.3 * l)
    nb, seq, _ = x_prompt.shape
    db, dseq, _ = x_sample.shape

    cond = jnp.concatenate([c_ctx[None, :], c, jnp.zeros((SUBLANES - 1 - db, D_MODEL), F32)], axis=0)
    mod = _ada(cond, w_ada[l], b_ada[l][None, :])
    mod_ctx = mod[0:1].reshape(1, 1, 6 * D_MODEL)
    mod_lat = mod[1:1 + db].reshape(db, 1, 6 * D_MODEL)

    w = jnp.transpose(w_in[l])
    w_gates = w[W_G0:W_G0 + 2 * D_MODEL]
    w_dt = jnp.pad(w[W_DT0:W_G0], ((0, LANES - 2 * SSM_HEADS), (0, 0)))
    wa = w_branch_a[l].astype(BF16)
    wb = w_branch_b[l].astype(BF16)
    wo = w_out[l].astype(BF16)
    wg = w_ffn_gate[l].astype(BF16)
    wu = w_ffn_up[l].astype(BF16)
    wd = w_ffn_down[l].astype(BF16)
    g1 = norm1_g[l][None, :]
    g2 = norm2_g[l][None, :]
    qg = jnp.tile(q_norm_g[l], 2)[None, :]
    kg = jnp.tile(k_norm_g[l], 2)[None, :]
    sg = attn_sub_g[l][None, :]
    lamp = jnp.stack([lambda_q1[l], lambda_k1[l], lambda_q2[l], lambda_k2[l]], axis=0)
    bias_row = _pad_lanes(dt_bias[l])
    alog_row = _pad_lanes(A_log[l])
    dexp = jnp.repeat(D_skip[l], SSM_HEADDIM)[None, :]
    cw = conv_w[l]
    cb = conv_b[l][None, :]
    ssm_g = ssm_norm_g[l][None, :]

    def layer(x, mod_rows, rope_tabs, ctx, h0, emit_kv):
        batch, sl, _ = x.shape
        x2d = x.reshape(batch * sl, D_MODEL)
        h, dt_raw = _norm1(x2d, mod_rows, g1, w_dt, tm=512)
        p2d = _proj(h, w, w_gates, tm=2048)
        cs, rtT = _dtprep(dt_raw, bias_row, alog_row)
        hps = max(1, min(ATTN_HEADS, (4096 if ctx is not None else 2048) // sl))
        attn_out = _attn(p2d, lamp, qg, kg, sg, batch, sl, lam_init, hps, rope_tabs, ctx, emit_kv)
        gps = max(SSD_GROUPS_PER_STEP, min(SSM_GROUPS, 1024 // sl))
        y, hf, hb = _ssd(p2d, cs, rtT, cw, cb, dexp, batch, sl, gps, h0)
        x1 = _merge(x2d, mod_rows, attn_out[0], y, p2d, ssm_g, wa, wb, wo, tm=512)
        x2 = _ffn(x1, mod_rows, g2, wg, wu, wd, tm=512)
        return x2.reshape(batch, sl, D_MODEL), attn_out[1:], hf, hb

    y_prompt, kv, hf, hb = layer(x_prompt, mod_ctx, None, None, None, True)
    ctx_kT = jnp.transpose(cache_k[:, l], (0, 2, 3, 4, 1)).reshape(db, ATTN_HEADS, HEAD_W, -1)
    ctx_v = cache_v[:, l].reshape(db, -1, ATTN_VD)
    h0 = (state_ssm_fwd[:, l].reshape(db, D_INNER, SSM_STATE),
          state_ssm_bwd[:, l].reshape(db, D_INNER, SSM_STATE))
    y_sample, _, _, _ = layer(x_sample, mod_lat, _rope_tables(dseq), (ctx_kT, ctx_v), h0, False)

    knT = kv[0].reshape(nb, ATTN_HEADS, 2, ATTN_DH, seq)
    new_cache_k = jnp.transpose(knT, (0, 4, 1, 2, 3)).reshape(nb, 1, seq, ATTN_HEADS, 2, ATTN_DH)
    new_cache_v = kv[1].reshape(nb, 1, seq, ATTN_HEADS, ATTN_VD)
    new_hf = hf.reshape(nb, 1, SSM_HEADS, SSM_HEADDIM, SSM_STATE)
    new_hb = hb.reshape(nb, 1, SSM_HEADS, SSM_HEADDIM, SSM_STATE)
    return (y_prompt, y_sample, new_cache_k, new_cache_v, new_hf, new_hb)
```

```python
import functools
import math

import jax
import jax.numpy as jnp
from jax import lax
from jax.experimental import pallas as pl
from jax.experimental.pallas import tpu as pltpu

F32 = jnp.float32
BF16 = jnp.bfloat16

D_MODEL = 1024
GRID_W = 64
ATTN_HEADS = 8
ATTN_DH = 64
ATTN_VD = 128
HEAD_W = 2 * ATTN_DH
ROPE_BASE = 10000.0
ROPE_AXIS_DIM = ATTN_DH // 2
ROPE_HALF = ROPE_AXIS_DIM // 2
D_INNER = 2 * D_MODEL
SSM_HEADDIM = 64
SSM_HEADS = D_INNER // SSM_HEADDIM
SSM_GROUPS = 8
HEADS_PER_GROUP = SSM_HEADS // SSM_GROUPS
GROUP_W = HEADS_PER_GROUP * SSM_HEADDIM
SSM_STATE = 128
D_CONV = 5
CONV_PAD = 8
CONV_EDGE = 16
CHUNK = 128
D_FF = ((8 * D_MODEL // 3 + 255) // 256) * 256
EPS = 1e-6
LANES = 128
SUBLANES = 8
LOG2E = math.log2(math.e)

P_Z = 0
P_GA = P_Z + D_INNER
P_GB = P_GA + D_MODEL
P_X = P_GB + D_MODEL
P_B = P_X + D_INNER
P_C = P_B + SSM_GROUPS * SSM_STATE
P_Q = P_C + SSM_GROUPS * SSM_STATE
P_K = P_Q + D_MODEL
P_V = P_K + D_MODEL
NP = P_V + D_MODEL
PROJ_TN = 1024
ACT_DTYPE = BF16

W_Q0 = 0
W_Z0 = 3 * D_MODEL
W_XBC0 = W_Z0 + D_INNER
W_DT0 = W_XBC0 + D_INNER + 2 * SSM_GROUPS * SSM_STATE
W_G0 = W_DT0 + 2 * SSM_HEADS

VMEM_LIMIT = 56 * 1024 * 1024


def _cparams(sem):
    return pltpu.CompilerParams(dimension_semantics=sem, vmem_limit_bytes=VMEM_LIMIT)


def _silu(x):
    return x * jax.nn.sigmoid(x)


def _dot(a, b):
    return jnp.dot(a, b, preferred_element_type=F32)


def _dot_nt(a, b):
    return lax.dot_general(a, b, (((1,), (1,)), ((), ())), preferred_element_type=F32)


def _split3(x):
    hi = x.astype(BF16)
    r = x - hi.astype(F32)
    mid = r.astype(BF16)
    lo = (r - mid.astype(F32)).astype(BF16)
    return hi, mid, lo


def _resident(shape):
    return pl.BlockSpec(shape, lambda *_: (0,) * len(shape), pipeline_mode=pl.Buffered(1))


def _ada_kernel(cond_ref, w_ref, b_ref, o_ref):
    s = _silu(cond_ref[...]).astype(BF16)
    o_ref[...] = _dot(s, w_ref[...].astype(BF16)) + b_ref[...]


def _ada(cond, w_ada, b_ada):
    rows = cond.shape[0]
    n = w_ada.shape[1]
    tn = 1024
    return pl.pallas_call(
        _ada_kernel,
        grid=(n // tn,),
        in_specs=[pl.BlockSpec((rows, D_MODEL), lambda j: (0, 0)),
                  pl.BlockSpec((D_MODEL, tn), lambda j: (0, j)),
                  pl.BlockSpec((1, tn), lambda j: (0, j))],
        out_specs=pl.BlockSpec((rows, tn), lambda j: (0, j)),
        out_shape=jax.ShapeDtypeStruct((rows, n), F32),
        compiler_params=_cparams(("arbitrary",)),
        name="ada",
    )(cond, w_ada, b_ada)


def _norm1_kernel(x_ref, mod_ref, g_ref, wdt_ref, h_ref, dt_ref):
    x = x_ref[...]
    y = x * lax.rsqrt(jnp.mean(x * x, axis=-1, keepdims=True) + EPS) * g_ref[...]
    shift = mod_ref[0, :, 0:D_MODEL]
    scale = mod_ref[0, :, D_MODEL:2 * D_MODEL]
    h = (y * (1.0 + scale) + shift).astype(BF16)
    h_ref[...] = h
    dt_ref[...] = _dot_nt(h, wdt_ref[...].astype(BF16))


def _norm1(x2d, mod, g1, w_dt, tm):
    m = x2d.shape[0]
    rows_per_mod = m // mod.shape[0]
    return pl.pallas_call(
        _norm1_kernel,
        grid=(m // tm,),
        in_specs=[pl.BlockSpec((tm, D_MODEL), lambda i: (i, 0)),
                  pl.BlockSpec((1, 1, 6 * D_MODEL), lambda i: ((i * tm) // rows_per_mod, 0, 0)),
                  _resident((1, D_MODEL)),
                  _resident((LANES, D_MODEL))],
        out_specs=[pl.BlockSpec((tm, D_MODEL), lambda i: (i, 0)),
                   pl.BlockSpec((tm, LANES), lambda i: (i, 0))],
        out_shape=[jax.ShapeDtypeStruct((m, D_MODEL), BF16),
                   jax.ShapeDtypeStruct((m, LANES), F32)],
        compiler_params=_cparams(("arbitrary",)),
        name="norm1",
    )(x2d, mod, g1, w_dt)


_GATE_TILE0 = P_GA // PROJ_TN
_N_GATE_TILES = 2 * D_MODEL // PROJ_TN


def _w_in_tile(j):
    z_tiles = D_INNER // PROJ_TN
    xbc_tile0 = (P_X // PROJ_TN)
    qkv_tile0 = (P_Q // PROJ_TN)
    z_src = jnp.minimum(j, z_tiles - 1) + W_Z0 // PROJ_TN
    xbc_src = j - xbc_tile0 + W_XBC0 // PROJ_TN
    qkv_src = j - qkv_tile0 + W_Q0 // PROJ_TN
    return jnp.where(j < xbc_tile0, z_src, jnp.where(j < qkv_tile0, xbc_src, qkv_src))


def _proj_kernel(h_ref, win_ref, wg_ref, p_ref, w_scr, *, tm):
    j = pl.program_id(0)
    i = pl.program_id(1)
    is_gate = jnp.logical_and(j >= _GATE_TILE0, j < _GATE_TILE0 + _N_GATE_TILES)

    @pl.when(jnp.logical_and(i == 0, is_gate))
    def _():
        w_scr[...] = wg_ref[...].astype(BF16)

    @pl.when(jnp.logical_and(i == 0, jnp.logical_not(is_gate)))
    def _():
        w_scr[...] = win_ref[...].astype(BF16)

    r0 = pl.multiple_of(i * tm, tm)
    p_ref[...] = _dot_nt(h_ref[pl.ds(r0, tm), :], w_scr[...]).astype(p_ref.dtype)


def _proj(h, w_in_t, w_gates_t, tm):
    m = h.shape[0]
    tn = PROJ_TN
    return pl.pallas_call(
        functools.partial(_proj_kernel, tm=tm),
        grid=(NP // tn, m // tm),
        in_specs=[_resident((m, D_MODEL)),
                  pl.BlockSpec((tn, D_MODEL), lambda j, i: (_w_in_tile(j), 0)),
                  pl.BlockSpec((tn, D_MODEL),
                               lambda j, i: (jnp.clip(j - _GATE_TILE0, 0, _N_GATE_TILES - 1), 0))],
        out_specs=pl.BlockSpec((tm, tn), lambda j, i: (i, j)),
        out_shape=jax.ShapeDtypeStruct((m, NP), ACT_DTYPE),
        scratch_shapes=[pltpu.VMEM((tn, D_MODEL), BF16)],
        compiler_params=_cparams(("arbitrary", "arbitrary")),
        name="proj",
    )(h, w_in_t, w_gates_t)


DTPREP_CHUNKS = 4
SSD_GROUPS_PER_STEP = 4

def _dtprep_kernel(dtr_ref, bias_ref, alog_ref, cs_ref, rtT_ref):
    row = lax.broadcasted_iota(jnp.int32, (CHUNK, CHUNK), 0)
    col = lax.broadcasted_iota(jnp.int32, (CHUNK, CHUNK), 1)
    tril = jnp.where(col <= row, 1.0, 0.0).astype(BF16)
    triu = jnp.where(col >= row, 1.0, 0.0).astype(BF16)
    neg_a = -jnp.exp(alog_ref[...])
    for c in range(DTPREP_CHUNKS):
        sl = slice(c * CHUNK, (c + 1) * CHUNK)
        x = dtr_ref[sl, :] + bias_ref[...]
        dt = jnp.maximum(x, 0.0) + jnp.log1p(jnp.exp(-jnp.abs(x)))
        hi, mid, lo = _split3(dt * neg_a)
        cf = _dot(tril, hi) + _dot(tril, mid) + _dot(tril, lo)
        cb = _dot(triu, hi) + _dot(triu, mid) + _dot(triu, lo)
        cs = jnp.where(col < SSM_HEADS, cf, cb)
        cs_ref[sl, :] = cs * LOG2E
        rtT_ref[c] = ((cs - jnp.log(dt)) * LOG2E).T


def _dtprep(dt_raw, bias_row, alog_row):
    m = dt_raw.shape[0]
    rows = DTPREP_CHUNKS * CHUNK
    return pl.pallas_call(
        _dtprep_kernel,
        grid=(m // rows,),
        in_specs=[pl.BlockSpec((rows, LANES), lambda i: (i, 0)),
                  _resident((1, LANES)),
                  _resident((1, LANES))],
        out_specs=[pl.BlockSpec((rows, LANES), lambda i: (i, 0)),
                   pl.BlockSpec((DTPREP_CHUNKS, CHUNK, CHUNK), lambda i: (i, 0, 0))],
        out_shape=[jax.ShapeDtypeStruct((m, LANES), F32),
                   jax.ShapeDtypeStruct((m // CHUNK, CHUNK, CHUNK), F32)],
        compiler_params=_cparams(("arbitrary",)),
        name="dtprep",
    )(dt_raw, bias_row, alog_row)


def _half_norm(x, g, half_ones):
    ss = x * x
    hi = ss.astype(BF16)
    lo = (ss - hi.astype(F32)).astype(BF16)
    ms = _dot(jnp.concatenate([hi, lo], axis=1), half_ones) * (1.0 / ATTN_DH)
    return x * lax.rsqrt(ms + EPS) * g


def _rope(x, cos, sin_up, sin_dn):
    x_up = pltpu.roll(x, HEAD_W - ROPE_HALF, axis=1)
    x_dn = pltpu.roll(x, ROPE_HALF, axis=1)
    return x * cos + x_up * sin_up + x_dn * sin_dn


def _attn_kernel(*refs, lq, lctx, hps, lam_init, rope, emit_kv):
    it = iter(refs)
    lamp_ref, qg_ref, kg_ref, sg_ref, q_ref, k_ref, v_ref = (next(it) for _ in range(7))
    if rope:
        cos_ref, sup_ref, sdn_ref = next(it), next(it), next(it)
    if lctx:
        ckT_ref, cv_ref = next(it), next(it)
    o_ref = next(it)
    if emit_kv:
        knT_ref, vo_ref = next(it), next(it)
    head_scr = [tuple(next(it) for _ in range(4)) for _ in range(hps)]
    if lctx:
        kcT_scrs = [next(it) for _ in range(hps)]

    lp = lamp_ref[...]
    lam = (jnp.exp(jnp.sum(lp[0:1] * lp[1:2], axis=-1, keepdims=True))
           - jnp.exp(jnp.sum(lp[2:3] * lp[3:4], axis=-1, keepdims=True)) + lam_init)

    r_i = (lax.broadcasted_iota(jnp.int32, (2 * HEAD_W, HEAD_W), 0) // ATTN_DH) & 1
    c_i = lax.broadcasted_iota(jnp.int32, (2 * HEAD_W, HEAD_W), 1) // ATTN_DH
    half_ones = jnp.where(r_i == c_i, 1.0, 0.0).astype(BF16)

    first_half = lax.broadcasted_iota(jnp.int32, (lq, HEAD_W), 1) < ATTN_DH
    sub_gain = sg_ref[...] * (1.0 - lam_init)
    ones = jnp.ones((lq + lctx, ATTN_VD), BF16)

    def stage_ctx_values(va_scr, hh):
        head = pl.program_id(1) * hps + hh
        rows = cv_ref[0, pl.ds(head, lctx, stride=ATTN_HEADS), :]
        va_scr[lq:lq + lctx, 0:ATTN_VD] = rows.astype(BF16)

    for hh in range(hps):
        cols = slice(hh * HEAD_W, (hh + 1) * HEAD_W)
        qn = _half_norm(q_ref[:, cols].astype(F32), qg_ref[...], half_ones)
        kn = _half_norm(k_ref[:, cols].astype(F32), kg_ref[...], half_ones)
        v = v_ref[:, cols]
        if emit_kv:
            knT_ref[0, hh] = kn.T
            vo_ref[:, cols] = v.astype(F32)
        if rope:
            qn = _rope(qn, cos_ref[...], sup_ref[...], sdn_ref[...])
            kn = _rope(kn, cos_ref[...], sup_ref[...], sdn_ref[...])
        qs = qn * (ATTN_DH ** -0.5 * LOG2E)
        q1_scr, q2_scr, kk_scr, va_scr = head_scr[hh]
        q1_scr[...] = jnp.where(first_half, qs, 0.0).astype(BF16)
        q2_scr[...] = jnp.where(first_half, 0.0, qs).astype(BF16)
        kk_scr[...] = kn.astype(BF16)
        va_scr[0:lq, 0:ATTN_VD] = v.astype(BF16)
        va_scr[:, ATTN_VD:2 * ATTN_VD] = ones
        if lctx:
            kcT_scrs[hh][...] = ckT_ref[0, hh].astype(BF16)
            stage_ctx_values(va_scr, hh)

        def softmax_pv(qb, kk_scr=kk_scr, va_scr=va_scr, hh=hh):
            s = _dot_nt(qb, kk_scr[...])
            if lctx:
                s = jnp.concatenate([s, _dot(qb, kcT_scrs[hh][...])], axis=1)
            p = jnp.exp2(s - jnp.max(s, axis=-1, keepdims=True)).astype(BF16)
            a = _dot(p, va_scr[...])
            return a[:, 0:ATTN_VD] / a[:, ATTN_VD:2 * ATTN_VD]

        o = softmax_pv(q1_scr[...]) - lam * softmax_pv(q2_scr[...])
        o = o * lax.rsqrt(jnp.mean(o * o, axis=-1, keepdims=True) + EPS) * sub_gain
        o_ref[:, cols] = o.astype(o_ref.dtype)


def _attn(p2d, lamp, qg, kg, sg, batch, lq, lam_init, hps, rope_tabs=None, ctx=None, emit_kv=False):
    m = p2d.shape[0]
    lctx = 0 if ctx is None else ctx[0].shape[-1]
    width = hps * HEAD_W
    qb, kb, vb = P_Q // width, P_K // width, P_V // width
    const = lambda b, h: (0, 0)
    in_specs = [pl.BlockSpec((4, ATTN_DH), const),
                pl.BlockSpec((1, HEAD_W), const),
                pl.BlockSpec((1, HEAD_W), const),
                pl.BlockSpec((1, ATTN_VD), const),
                pl.BlockSpec((lq, width), lambda b, h: (b, qb + h)),
                pl.BlockSpec((lq, width), lambda b, h: (b, kb + h)),
                pl.BlockSpec((lq, width), lambda b, h: (b, vb + h))]
    args = [lamp, qg, kg, sg, p2d, p2d, p2d]
    if rope_tabs is not None:
        in_specs += [pl.BlockSpec((lq, HEAD_W), const)] * 3
        args += list(rope_tabs)
    if ctx is not None:
        in_specs += [pl.BlockSpec((1, hps, HEAD_W, lctx), lambda b, h: (b, h, 0, 0)),
                     pl.BlockSpec((1, lctx * ATTN_HEADS, ATTN_VD), lambda b, h: (b, 0, 0))]
        args += list(ctx)
    head_out = pl.BlockSpec((lq, width), lambda b, h: (b, h))
    out_specs = [head_out]
    out_shape = [jax.ShapeDtypeStruct((m, D_MODEL), ACT_DTYPE)]
    if emit_kv:
        out_specs += [pl.BlockSpec((1, hps, HEAD_W, lq), lambda b, h: (b, h, 0, 0)), head_out]
        out_shape += [jax.ShapeDtypeStruct((batch, ATTN_HEADS, HEAD_W, lq), F32),
                      jax.ShapeDtypeStruct((m, D_MODEL), F32)]
    scratch = hps * [pltpu.VMEM((lq, HEAD_W), BF16),
                     pltpu.VMEM((lq, HEAD_W), BF16),
                     pltpu.VMEM((lq, HEAD_W), BF16),
                     pltpu.VMEM((lq + lctx, 2 * ATTN_VD), BF16)]
    if lctx:
        scratch += hps * [pltpu.VMEM((HEAD_W, lctx), BF16)]
    kern = functools.partial(_attn_kernel, lq=lq, lctx=lctx, hps=hps, lam_init=lam_init,
                             rope=rope_tabs is not None, emit_kv=emit_kv)
    return pl.pallas_call(
        kern,
        grid=(batch, ATTN_HEADS // hps),
        in_specs=in_specs,
        out_specs=out_specs,
        out_shape=out_shape,
        scratch_shapes=scratch,
        compiler_params=_cparams(("arbitrary", "arbitrary")),
        name="attn",
    )(*args)


def _ssd_kernel(*refs, seq, has_h0, gps):
    it = iter(refs)
    x_ref, b_ref, c_ref = next(it), next(it), next(it)
    cwx_ref, cwb_ref, cwc_ref = next(it), next(it), next(it)
    cbx_ref, cbb_ref, cbc_ref = next(it), next(it), next(it)
    dexp_ref, cs_ref, rtT_ref = next(it), next(it), next(it)
    if has_h0:
        h0f_ref, h0b_ref = next(it), next(it)
    y_ref, hf_ref, hb_ref = next(it), next(it), next(it)
    group_scr = [tuple(next(it) for _ in range(10)) for _ in range(gps)]

    g0 = pl.program_id(1) * gps
    nc = seq // CHUNK
    x_lanes = [slice(gi * GROUP_W, (gi + 1) * GROUP_W) for gi in range(gps)]
    n_lanes = [slice(gi * SSM_STATE, (gi + 1) * SSM_STATE) for gi in range(gps)]
    blk = lax.broadcasted_iota(jnp.int32, (CHUNK, GROUP_W), 1) // SSM_HEADDIM

    win_rows = CHUNK + 2 * CONV_EDGE
    sel_r = lax.broadcasted_iota(jnp.int32, (CHUNK, win_rows), 0)
    sel_c = lax.broadcasted_iota(jnp.int32, (CHUNK, win_rows), 1)
    taps = [d for d in range(-(D_CONV // 2), D_CONV // 2 + 1) if d != 0]
    shift_sel = {d: jnp.where(sel_c == sel_r + CONV_EDGE + d, 1.0, 0.0).astype(BF16) for d in taps}

    def conv_chunk(c, sources, shift_on_mxu):
        r0 = pl.multiple_of(c * CHUNK, CHUNK)
        p0 = pl.multiple_of(jnp.maximum(r0 - CONV_EDGE, 0), CONV_EDGE)
        n0 = pl.multiple_of(jnp.minimum(r0 + CHUNK, seq - CONV_EDGE), CONV_EDGE)

        def gather(rows_of):
            parts = [rows_of(*src) for src in sources]
            return parts[0] if len(parts) == 1 else jnp.concatenate(parts, axis=1)

        assert all(src[0].dtype == BF16 for src in sources)
        prev = gather(lambda ref, w_ref, b_ref, lanes: ref[pl.ds(p0, CONV_EDGE), lanes])
        nxt = gather(lambda ref, w_ref, b_ref, lanes: ref[pl.ds(n0, CONV_EDGE), lanes])
        main = gather(lambda ref, w_ref, b_ref, lanes: ref[pl.ds(r0, CHUNK), lanes])
        w = [gather(lambda ref, w_ref, b_ref, lanes, j=j: w_ref[j:j + 1, lanes]) for j in range(D_CONV)]
        bias = gather(lambda ref, w_ref, b_ref, lanes: b_ref[:, lanes])
        acc = bias + main.astype(F32) * w[D_CONV // 2]
        if shift_on_mxu:
            zero = jnp.zeros_like(prev)
            win = jnp.concatenate([jnp.where(c > 0, prev, zero), main,
                                   jnp.where(c < nc - 1, nxt, zero)], axis=0)
            for d in taps:
                acc = acc + _dot(shift_sel[d], win) * w[d + D_CONV // 2]
        else:
            rows = CHUNK + 2 * CONV_PAD
            win = jnp.concatenate(
                [jnp.where(c > 0, prev.astype(F32)[CONV_EDGE - CONV_PAD:, :], 0.0), main.astype(F32),
                 jnp.where(c < nc - 1, nxt.astype(F32)[:CONV_PAD, :], 0.0)], axis=0)
            for d in taps:
                tap = pltpu.roll(win, (-d) % rows, axis=0)[CONV_PAD:CONV_PAD + CHUNK, :]
                acc = acc + tap * w[d + D_CONV // 2]
        return _silu(acc)

    head_masks = [jnp.where(blk == j, 1.0, 0.0).astype(BF16) for j in range(HEADS_PER_GROUP)]

    def conv_body(c, carry):
        sl = pl.ds(pl.multiple_of(c * CHUNK, CHUNK), CHUNK)
        for gi in range(gps):
            xm_s, bT_s, cc_s, _, _, _, yacc, g_s, _, _ = group_scr[gi]
            xv = conv_chunk(c, [(x_ref, cwx_ref, cbx_ref, x_lanes[gi])], nc > 2)
            yacc[sl, :] = dexp_ref[:, x_lanes[gi]] * xv
            xb = xv.astype(BF16)
            for j in range(HEADS_PER_GROUP):
                xm_s[c, j * CHUNK:(j + 1) * CHUNK, :] = xb * head_masks[j]
            bc = conv_chunk(c, [(b_ref, cwb_ref, cbb_ref, n_lanes[gi]),
                                (c_ref, cwc_ref, cbc_ref, n_lanes[gi])], False)
            bT = bc[:, 0:SSM_STATE].T.astype(BF16)
            cc = bc[:, SSM_STATE:2 * SSM_STATE].astype(BF16)
            bT_s[c] = bT
            cc_s[sl, :] = cc
            g_s[c] = _dot(cc, bT)
        return carry

    lax.fori_loop(0, nc, conv_body, 0, unroll=2)

    for gi in range(gps):
        csg_s, stf, stb = group_scr[gi][3:6]
        shift = (LANES - HEADS_PER_GROUP * (g0 + gi)) % LANES
        csg_s[...] = pltpu.roll(cs_ref[...], shift, axis=1)
        if has_h0:
            stf[...] = h0f_ref[0, x_lanes[gi], :].T
            stb[...] = h0b_ref[0, x_lanes[gi], :].T
        else:
            stf[...] = jnp.zeros_like(stf)
            stb[...] = jnp.zeros_like(stb)

    row = lax.broadcasted_iota(jnp.int32, (CHUNK, CHUNK), 0)
    col = lax.broadcasted_iota(jnp.int32, (CHUNK, CHUNK), 1)
    lane_lo = lax.broadcasted_iota(jnp.int32, (CHUNK, LANES), 1) < SSM_HEADDIM

    def decay_terms(c, gi):
        xm_s, bT_s, cc_s, csg_s, _, _, yacc, g_s, upd_s, e_s = group_scr[gi]
        g = g0 + gi
        sl = pl.ds(pl.multiple_of(c * CHUNK, CHUNK), CHUNK)
        bT = bT_s[c]
        cs = csg_s[sl, :]
        x_heads = xm_s[c]
        dec_sum = [None] * HEADS_PER_GROUP
        for d, (lane0, mask, edge) in enumerate(((0, col <= row, CHUNK - 1), (SSM_HEADS, col >= row, 0))):
            upd_lhs, cols = [], []
            for j in range(HEADS_PER_GROUP):
                cb = jnp.broadcast_to(cs[:, lane0 + j:lane0 + j + 1], (CHUNK, CHUNK))
                rrow = rtT_ref[c, pl.ds(lane0 + HEADS_PER_GROUP * g + j, 1), :]
                dec = jnp.exp2(jnp.where(mask, cb - rrow, -jnp.inf))
                dec_sum[j] = dec if dec_sum[j] is None else dec_sum[j] + dec
                w_edge = dec[edge:edge + 1, :]
                upd_lhs.append(bT * w_edge.astype(BF16))
                cols.append(cb)
            upd_s[d, c] = _dot(jnp.concatenate(upd_lhs, axis=1), x_heads)
            cs_e = jnp.concatenate([jnp.where(lane_lo, cols[2 * t], cols[2 * t + 1])
                                    for t in range(HEADS_PER_GROUP // 2)], axis=1)
            e_s[d, c] = jnp.exp2(cs_e)
        gmat = g_s[c]
        diag_lhs = [(gmat * dec_sum[j]).astype(BF16) for j in range(HEADS_PER_GROUP)]
        yacc[sl, :] = yacc[sl, :] + _dot(jnp.concatenate(diag_lhs, axis=1), x_heads)

    def decay_body(c, carry):
        for gi in range(gps):
            decay_terms(c, gi)
        return carry

    lax.fori_loop(0, nc, decay_body, 0, unroll=2)

    def state_step(c, d, gi):
        _, _, cc_s, _, stf, stb, yacc, _, upd_s, e_s = group_scr[gi]
        st_ref = stb if d else stf
        edge = 0 if d else CHUNK - 1
        sl = pl.ds(pl.multiple_of(c * CHUNK, CHUNK), CHUNK)
        e = e_s[d, c]
        st = st_ref[...]
        yacc[sl, :] = yacc[sl, :] + e * _dot(cc_s[sl, :], st.astype(BF16))
        st_ref[...] = e[edge:edge + 1, :] * st + upd_s[d, c]

    def state_body(i, carry):
        for gi in range(gps):
            state_step(i, 0, gi)
            state_step(nc - 1 - i, 1, gi)
        return carry

    lax.fori_loop(0, nc, state_body, 0, unroll=2)

    for gi in range(gps):
        stf, stb, yacc = group_scr[gi][4:7]
        y_ref[:, x_lanes[gi]] = yacc[...].astype(y_ref.dtype)
        hf_ref[0, x_lanes[gi], :] = stf[...].T
        hb_ref[0, x_lanes[gi], :] = stb[...].T


def _ssd(p2d, cs, rtT, conv_w, conv_b, dexp, batch, seq, gps, h0=None):
    m = p2d.shape[0]
    nc = seq // CHUNK
    xw, nw = gps * GROUP_W, gps * SSM_STATE
    xb, bb, cb = P_X // xw, P_B // nw, P_C // nw
    cw_b0 = D_INNER // nw
    cw_c0 = cw_b0 + SSM_GROUPS // gps
    in_specs = [pl.BlockSpec((seq, xw), lambda b, g: (b, xb + g)),
                pl.BlockSpec((seq, nw), lambda b, g: (b, bb + g)),
                pl.BlockSpec((seq, nw), lambda b, g: (b, cb + g)),
                pl.BlockSpec((D_CONV, xw), lambda b, g: (0, g)),
                pl.BlockSpec((D_CONV, nw), lambda b, g: (0, cw_b0 + g)),
                pl.BlockSpec((D_CONV, nw), lambda b, g: (0, cw_c0 + g)),
                pl.BlockSpec((1, xw), lambda b, g: (0, g)),
                pl.BlockSpec((1, nw), lambda b, g: (0, cw_b0 + g)),
                pl.BlockSpec((1, nw), lambda b, g: (0, cw_c0 + g)),
                pl.BlockSpec((1, xw), lambda b, g: (0, g)),
                pl.BlockSpec((seq, LANES), lambda b, g: (b, 0)),
                pl.BlockSpec((nc, CHUNK, CHUNK), lambda b, g: (b, 0, 0))]
    args = [p2d, p2d, p2d, conv_w, conv_w, conv_w, conv_b, conv_b, conv_b, dexp, cs, rtT]
    state_spec = pl.BlockSpec((1, xw, SSM_STATE), lambda b, g: (b, g, 0))
    if h0 is not None:
        in_specs += [state_spec, state_spec]
        args += list(h0)
    state_shape = jax.ShapeDtypeStruct((batch, D_INNER, SSM_STATE), F32)
    kern = functools.partial(_ssd_kernel, seq=seq, has_h0=h0 is not None, gps=gps)
    return pl.pallas_call(
        kern,
        grid=(batch, SSM_GROUPS // gps),
        in_specs=in_specs,
        out_specs=[pl.BlockSpec((seq, xw), lambda b, g: (b, g)), state_spec, state_spec],
        out_shape=[jax.ShapeDtypeStruct((m, D_INNER), ACT_DTYPE), state_shape, state_shape],
        scratch_shapes=gps * [pltpu.VMEM((nc, HEADS_PER_GROUP * CHUNK, GROUP_W), BF16),
                              pltpu.VMEM((nc, SSM_STATE, CHUNK), BF16),
                              pltpu.VMEM((seq, SSM_STATE), BF16),
                              pltpu.VMEM((seq, LANES), F32),
                              pltpu.VMEM((SSM_STATE, GROUP_W), F32),
                              pltpu.VMEM((SSM_STATE, GROUP_W), F32),
                              pltpu.VMEM((seq, GROUP_W), F32),
                              pltpu.VMEM((nc, CHUNK, CHUNK), F32),
                              pltpu.VMEM((2, nc, SSM_STATE, GROUP_W), F32),
                              pltpu.VMEM((2, nc, CHUNK, GROUP_W), F32)],
        compiler_params=_cparams(("arbitrary", "arbitrary")),
        name="ssd",
    )(*args)


def _merge_kernel(x_ref, mod_ref, o_ref, y_ref, z_ref, ga_ref, gb_ref, sg_ref,
                  wa_ref, wb_ref, wo_ref, out_ref):
    out_a = _dot(o_ref[...], wa_ref[...])
    yz = y_ref[...].astype(F32) * _silu(z_ref[...].astype(F32))
    yn = yz * lax.rsqrt(jnp.mean(yz * yz, axis=-1, keepdims=True) + EPS) * sg_ref[...]
    out_b = _dot(yn.astype(BF16), wb_ref[...])
    merged = (jax.nn.sigmoid(ga_ref[...].astype(F32)) * out_a
              + jax.nn.sigmoid(gb_ref[...].astype(F32)) * out_b)
    gate1 = mod_ref[0, :, 2 * D_MODEL:3 * D_MODEL]
    out_ref[...] = x_ref[...] + gate1 * _dot(merged.astype(BF16), wo_ref[...])


def _merge(x2d, mod, o, y, p2d, ssm_g, wa, wb, wo, tm):
    m = x2d.shape[0]
    rows_per_mod = m // mod.shape[0]
    return pl.pallas_call(
        _merge_kernel,
        grid=(m // tm,),
        in_specs=[pl.BlockSpec((tm, D_MODEL), lambda i: (i, 0)),
                  pl.BlockSpec((1, 1, 6 * D_MODEL), lambda i: ((i * tm) // rows_per_mod, 0, 0)),
                  pl.BlockSpec((tm, D_MODEL), lambda i: (i, 0)),
                  pl.BlockSpec((tm, D_INNER), lambda i: (i, 0)),
                  pl.BlockSpec((tm, D_INNER), lambda i: (i, P_Z // D_INNER)),
                  pl.BlockSpec((tm, D_MODEL), lambda i: (i, P_GA // D_MODEL)),
                  pl.BlockSpec((tm, D_MODEL), lambda i: (i, P_GB // D_MODEL)),
                  _resident((1, D_INNER)),
                  _resident((D_MODEL, D_MODEL)),
                  _resident((D_INNER, D_MODEL)),
                  _resident((D_MODEL, D_MODEL))],
        out_specs=pl.BlockSpec((tm, D_MODEL), lambda i: (i, 0)),
        out_shape=jax.ShapeDtypeStruct((m, D_MODEL), F32),
        compiler_params=_cparams(("arbitrary",)),
        name="merge",
    )(x2d, mod, o, y, p2d, p2d, p2d, ssm_g, wa, wb, wo)


def _ffn_kernel(x_ref, mod_ref, g_ref, wg_ref, wu_ref, wd_ref, out_ref):
    x = x_ref[...]
    y = x * lax.rsqrt(jnp.mean(x * x, axis=-1, keepdims=True) + EPS) * g_ref[...]
    shift = mod_ref[0, :, 3 * D_MODEL:4 * D_MODEL]
    scale = mod_ref[0, :, 4 * D_MODEL:5 * D_MODEL]
    gate = mod_ref[0, :, 5 * D_MODEL:6 * D_MODEL]
    h = (y * (1.0 + scale) + shift).astype(BF16)
    f = (_silu(_dot(h, wg_ref[...])) * _dot(h, wu_ref[...])).astype(BF16)
    out_ref[...] = x + gate * _dot(f, wd_ref[...])


def _ffn(x2d, mod, g2, wg, wu, wd, tm):
    m = x2d.shape[0]
    rows_per_mod = m // mod.shape[0]
    return pl.pallas_call(
        _ffn_kernel,
        grid=(m // tm,),
        in_specs=[pl.BlockSpec((tm, D_MODEL), lambda i: (i, 0)),
                  pl.BlockSpec((1, 1, 6 * D_MODEL), lambda i: ((i * tm) // rows_per_mod, 0, 0)),
                  _resident((1, D_MODEL)),
                  _resident((D_MODEL, D_FF)),
                  _resident((D_MODEL, D_FF)),
                  _resident((D_FF, D_MODEL))],
        out_specs=pl.BlockSpec((tm, D_MODEL), lambda i: (i, 0)),
        out_shape=jax.ShapeDtypeStruct((m, D_MODEL), F32),
        compiler_params=_cparams(("arbitrary",)),
        name="ffn",
    )(x2d, mod, g2, wg, wu, wd)


NORM_ROWS = 1024
PROJ_ROWS = 2048
MIX_ROWS = 512
ATTN_STEP_ROWS = 2048
ATTN_STEP_ROWS_CTX = 4096


def _tiles(rows, n_mod, seq, has_ctx):
    rows_per_mod = rows // n_mod
    attn_rows = ATTN_STEP_ROWS_CTX if has_ctx else ATTN_STEP_ROWS
    return dict(
        norm_rows=min(NORM_ROWS, rows_per_mod),
        proj_rows=min(PROJ_ROWS, rows),
        mix_rows=min(MIX_ROWS, rows_per_mod),
        heads_per_step=max(1, min(ATTN_HEADS, attn_rows // seq)),
        groups_per_step=SSD_GROUPS_PER_STEP,
    )


def _rope_tables(n_tok):
    pos = jnp.arange(n_tok, dtype=jnp.int32)
    row = (pos // GRID_W).astype(F32)
    colp = (pos % GRID_W).astype(F32)
    inv = ROPE_BASE ** (-jnp.arange(0, ROPE_AXIS_DIM, 2, dtype=F32) / ROPE_AXIS_DIM)
    ang_r = row[:, None] * inv[None, :]
    ang_c = colp[:, None] * inv[None, :]
    zero = jnp.zeros_like(ang_r)
    cos = jnp.concatenate([jnp.cos(ang_r)] * 2 + [jnp.cos(ang_c)] * 2, axis=-1)
    sup = jnp.concatenate([-jnp.sin(ang_r), zero, -jnp.sin(ang_c), zero], axis=-1)
    sdn = jnp.concatenate([zero, jnp.sin(ang_r), zero, jnp.sin(ang_c)], axis=-1)
    return tuple(jnp.tile(t, (1, 2)) for t in (cos, sup, sdn))


def _pad_lanes(v):
    flat = v.reshape(1, -1).astype(F32)
    return jnp.pad(flat, ((0, 0), (0, LANES - flat.shape[1])))


def kernel(x_prompt, x_sample, c, cache_k, cache_v, state_ssm_fwd, state_ssm_bwd, c_ctx, norm1_g, norm2_g, w_ada, b_ada, w_in, q_norm_g, k_norm_g, lambda_q1, lambda_k1, lambda_q2, lambda_k2, attn_sub_g, conv_w, conv_b, A_log, dt_bias, D_skip, ssm_norm_g, w_branch_a, w_branch_b, w_out, w_ffn_gate, w_ffn_up, w_ffn_down):
    depth = norm1_g.shape[0]
    assert depth == 1, "single-layer kernel"
    l = 0
    lam_init = 0.8 - 0.6 * math.exp(-0.3 * l)
    nb, seq, _ = x_prompt.shape
    db, dseq, _ = x_sample.shape

    cond = jnp.concatenate([c_ctx[None, :], c, jnp.zeros((SUBLANES - 1 - db, D_MODEL), F32)], axis=0)
    mod = _ada(cond, w_ada[l], b_ada[l][None, :])
    mod_ctx = mod[0:1].reshape(1, 1, 6 * D_MODEL)
    mod_lat = mod[1:1 + db].reshape(db, 1, 6 * D_MODEL)

    w = jnp.transpose(w_in[l])
    w_gates = w[W_G0:W_G0 + 2 * D_MODEL]
    w_dt = jnp.pad(w[W_DT0:W_G0], ((0, LANES - 2 * SSM_HEADS), (0, 0)))
    wa = w_branch_a[l].astype(BF16)
    wb = w_branch_b[l].astype(BF16)
    wo = w_out[l].astype(BF16)
    wg = w_ffn_gate[l].astype(BF16)
    wu = w_ffn_up[l].astype(BF16)
    wd = w_ffn_down[l].astype(BF16)
    g1 = norm1_g[l][None, :]
    g2 = norm2_g[l][None, :]
    qg = jnp.tile(q_norm_g[l], 2)[None, :]
    kg = jnp.tile(k_norm_g[l], 2)[None, :]
    sg = attn_sub_g[l][None, :]
    lamp = jnp.stack([lambda_q1[l], lambda_k1[l], lambda_q2[l], lambda_k2[l]], axis=0)
    bias_row = _pad_lanes(dt_bias[l])
    alog_row = _pad_lanes(A_log[l])
    dexp = jnp.repeat(D_skip[l], SSM_HEADDIM)[None, :]
    cw = conv_w[l]
    cb = conv_b[l][None, :]
    ssm_g = ssm_norm_g[l][None, :]

    def layer(x, mod_rows, rope_tabs, ctx, h0, emit_kv):
        batch, sl, _ = x.shape
        x2d = x.reshape(batch * sl, D_MODEL)
        tiles = _tiles(batch * sl, mod_rows.shape[0], sl, ctx is not None)
        h, dt_raw = _norm1(x2d, mod_rows, g1, w_dt, tm=tiles["norm_rows"])
        p2d = _proj(h, w, w_gates, tm=tiles["proj_rows"])
        cs, rtT = _dtprep(dt_raw, bias_row, alog_row)
        attn_out = _attn(p2d, lamp, qg, kg, sg, batch, sl, lam_init, tiles["heads_per_step"],
                         rope_tabs, ctx, emit_kv)
        y, hf, hb = _ssd(p2d, cs, rtT, cw, cb, dexp, batch, sl, tiles["groups_per_step"], h0)
        x1 = _merge(x2d, mod_rows, attn_out[0], y, p2d, ssm_g, wa, wb, wo, tm=tiles["mix_rows"])
        x2 = _ffn(x1, mod_rows, g2, wg, wu, wd, tm=tiles["mix_rows"])
        return x2.reshape(batch, sl, D_MODEL), attn_out[1:], hf, hb

    y_prompt, kv, hf, hb = layer(x_prompt, mod_ctx, None, None, None, True)
    ctx_kT = jnp.transpose(cache_k[:, l], (0, 2, 3, 4, 1)).reshape(db, ATTN_HEADS, HEAD_W, -1)
    ctx_v = cache_v[:, l].reshape(db, -1, ATTN_VD)
    h0 = (state_ssm_fwd[:, l].reshape(db, D_INNER, SSM_STATE),
          state_ssm_bwd[:, l].reshape(db, D_INNER, SSM_STATE))
    y_sample, _, _, _ = layer(x_sample, mod_lat, _rope_tables(dseq), (ctx_kT, ctx_v), h0, False)

    knT = kv[0].reshape(nb, ATTN_HEADS, 2, ATTN_DH, seq)
    new_cache_k = jnp.transpose(knT, (0, 4, 1, 2, 3)).reshape(nb, 1, seq, ATTN_HEADS, 2, ATTN_DH)
    new_cache_v = kv[1].reshape(nb, 1, seq, ATTN_HEADS, ATTN_VD)
    new_hf = hf.reshape(nb, 1, SSM_HEADS, SSM_HEADDIM, SSM_STATE)
    new_hb = hb.reshape(nb, 1, SSM_HEADS, SSM_HEADDIM, SSM_STATE)
    return (y_prompt, y_sample, new_cache_k, new_cache_v, new_hf, new_hb)
```

```python
import functools
import math

import jax
import jax.numpy as jnp
from jax import lax
from jax.experimental import pallas as pl
from jax.experimental.pallas import tpu as pltpu

F32 = jnp.float32
BF16 = jnp.bfloat16

D_MODEL = 1024
GRID_W = 64
ATTN_HEADS = 8
ATTN_DH = 64
ATTN_VD = 128
HEAD_W = 2 * ATTN_DH
ROPE_BASE = 10000.0
ROPE_AXIS_DIM = ATTN_DH // 2
ROPE_HALF = ROPE_AXIS_DIM // 2
D_INNER = 2 * D_MODEL
SSM_HEADDIM = 64
SSM_HEADS = D_INNER // SSM_HEADDIM
SSM_GROUPS = 8
HEADS_PER_GROUP = SSM_HEADS // SSM_GROUPS
GROUP_W = HEADS_PER_GROUP * SSM_HEADDIM
SSM_STATE = 128
D_CONV = 5
CONV_PAD = 8
CONV_EDGE = 16
CHUNK = 128
D_FF = ((8 * D_MODEL // 3 + 255) // 256) * 256
EPS = 1e-6
LANES = 128
SUBLANES = 8
LOG2E = math.log2(math.e)

P_Z = 0
P_GA = P_Z + D_INNER
P_GB = P_GA + D_MODEL
P_X = P_GB + D_MODEL
P_B = P_X + D_INNER
P_C = P_B + SSM_GROUPS * SSM_STATE
P_Q = P_C + SSM_GROUPS * SSM_STATE
P_K = P_Q + D_MODEL
P_V = P_K + D_MODEL
NP = P_V + D_MODEL
PROJ_TN = 512
ACT_DTYPE = BF16

W_Q0 = 0
W_Z0 = 3 * D_MODEL
W_XBC0 = W_Z0 + D_INNER
W_DT0 = W_XBC0 + D_INNER + 2 * SSM_GROUPS * SSM_STATE
W_G0 = W_DT0 + 2 * SSM_HEADS

VMEM_LIMIT = 56 * 1024 * 1024


def _cparams(sem):
    return pltpu.CompilerParams(dimension_semantics=sem, vmem_limit_bytes=VMEM_LIMIT)


def _silu(x):
    return x * jax.nn.sigmoid(x)


def _dot(a, b):
    return jnp.dot(a, b, preferred_element_type=F32)


def _dot_nt(a, b):
    return lax.dot_general(a, b, (((1,), (1,)), ((), ())), preferred_element_type=F32)


def _split3(x):
    hi = x.astype(BF16)
    r = x - hi.astype(F32)
    mid = r.astype(BF16)
    lo = (r - mid.astype(F32)).astype(BF16)
    return hi, mid, lo


def _resident(shape):
    return pl.BlockSpec(shape, lambda *_: (0,) * len(shape), pipeline_mode=pl.Buffered(1))


def _ada_kernel(cond_ref, w_ref, b_ref, o_ref):
    s = _silu(cond_ref[...]).astype(BF16)
    o_ref[...] = _dot(s, w_ref[...].astype(BF16)) + b_ref[...]


def _ada(cond, w_ada, b_ada):
    rows = cond.shape[0]
    n = w_ada.shape[1]
    tn = 1024
    return pl.pallas_call(
        _ada_kernel,
        grid=(n // tn,),
        in_specs=[pl.BlockSpec((rows, D_MODEL), lambda j: (0, 0)),
                  pl.BlockSpec((D_MODEL, tn), lambda j: (0, j)),
                  pl.BlockSpec((1, tn), lambda j: (0, j))],
        out_specs=pl.BlockSpec((rows, tn), lambda j: (0, j)),
        out_shape=jax.ShapeDtypeStruct((rows, n), F32),
        compiler_params=_cparams(("arbitrary",)),
        name="ada",
    )(cond, w_ada, b_ada)


def _norm1_kernel(x_ref, mod_ref, g_ref, wdt_ref, h_ref, dt_ref):
    x = x_ref[...]
    y = x * lax.rsqrt(jnp.mean(x * x, axis=-1, keepdims=True) + EPS) * g_ref[...]
    shift = mod_ref[0, :, 0:D_MODEL]
    scale = mod_ref[0, :, D_MODEL:2 * D_MODEL]
    h = (y * (1.0 + scale) + shift).astype(BF16)
    h_ref[...] = h
    dt_ref[...] = _dot_nt(h, wdt_ref[...].astype(BF16))


def _norm1(x2d, mod, g1, w_dt, tm):
    m = x2d.shape[0]
    rows_per_mod = m // mod.shape[0]
    return pl.pallas_call(
        _norm1_kernel,
        grid=(m // tm,),
        in_specs=[pl.BlockSpec((tm, D_MODEL), lambda i: (i, 0)),
                  pl.BlockSpec((1, 1, 6 * D_MODEL), lambda i: ((i * tm) // rows_per_mod, 0, 0)),
                  _resident((1, D_MODEL)),
                  _resident((LANES, D_MODEL))],
        out_specs=[pl.BlockSpec((tm, D_MODEL), lambda i: (i, 0)),
                   pl.BlockSpec((tm, LANES), lambda i: (i, 0))],
        out_shape=[jax.ShapeDtypeStruct((m, D_MODEL), BF16),
                   jax.ShapeDtypeStruct((m, LANES), F32)],
        compiler_params=_cparams(("arbitrary",)),
        name="norm1",
    )(x2d, mod, g1, w_dt)


_GATE_TILE0 = P_GA // PROJ_TN
_N_GATE_TILES = 2 * D_MODEL // PROJ_TN


def _w_in_tile(j):
    z_tiles = D_INNER // PROJ_TN
    xbc_tile0 = (P_X // PROJ_TN)
    qkv_tile0 = (P_Q // PROJ_TN)
    z_src = jnp.minimum(j, z_tiles - 1) + W_Z0 // PROJ_TN
    xbc_src = j - xbc_tile0 + W_XBC0 // PROJ_TN
    qkv_src = j - qkv_tile0 + W_Q0 // PROJ_TN
    return jnp.where(j < xbc_tile0, z_src, jnp.where(j < qkv_tile0, xbc_src, qkv_src))


def _proj_kernel(h_ref, win_ref, wg_ref, p_ref, w_scr, *, tm):
    j = pl.program_id(0)
    i = pl.program_id(1)
    is_gate = jnp.logical_and(j >= _GATE_TILE0, j < _GATE_TILE0 + _N_GATE_TILES)

    @pl.when(jnp.logical_and(i == 0, is_gate))
    def _():
        w_scr[...] = wg_ref[...].astype(BF16)

    @pl.when(jnp.logical_and(i == 0, jnp.logical_not(is_gate)))
    def _():
        w_scr[...] = win_ref[...].astype(BF16)

    r0 = pl.multiple_of(i * tm, tm)
    p_ref[...] = _dot_nt(h_ref[pl.ds(r0, tm), :], w_scr[...]).astype(p_ref.dtype)


def _proj(h, w_in_t, w_gates_t, tm):
    m = h.shape[0]
    tn = PROJ_TN
    return pl.pallas_call(
        functools.partial(_proj_kernel, tm=tm),
        grid=(NP // tn, m // tm),
        in_specs=[_resident((m, D_MODEL)),
                  pl.BlockSpec((tn, D_MODEL), lambda j, i: (_w_in_tile(j), 0)),
                  pl.BlockSpec((tn, D_MODEL),
                               lambda j, i: (jnp.clip(j - _GATE_TILE0, 0, _N_GATE_TILES - 1), 0))],
        out_specs=pl.BlockSpec((tm, tn), lambda j, i: (i, j)),
        out_shape=jax.ShapeDtypeStruct((m, NP), ACT_DTYPE),
        scratch_shapes=[pltpu.VMEM((tn, D_MODEL), BF16)],
        compiler_params=_cparams(("arbitrary", "arbitrary")),
        name="proj",
    )(h, w_in_t, w_gates_t)


DTPREP_CHUNKS = 8
SSD_GROUPS_PER_STEP = 4

def _dtprep_kernel(dtr_ref, bias_ref, alog_ref, cs_ref, rtT_ref):
    row = lax.broadcasted_iota(jnp.int32, (CHUNK, CHUNK), 0)
    col = lax.broadcasted_iota(jnp.int32, (CHUNK, CHUNK), 1)
    tril = jnp.where(col <= row, 1.0, 0.0).astype(BF16)
    triu = jnp.where(col >= row, 1.0, 0.0).astype(BF16)
    neg_a = -jnp.exp(alog_ref[...])
    for c in range(DTPREP_CHUNKS):
        sl = slice(c * CHUNK, (c + 1) * CHUNK)
        x = dtr_ref[sl, :] + bias_ref[...]
        dt = jnp.maximum(x, 0.0) + jnp.log1p(jnp.exp(-jnp.abs(x)))
        hi, mid, lo = _split3(dt * neg_a)
        cf = _dot(tril, hi) + _dot(tril, mid) + _dot(tril, lo)
        cb = _dot(triu, hi) + _dot(triu, mid) + _dot(triu, lo)
        cs = jnp.where(col < SSM_HEADS, cf, cb)
        cs_ref[sl, :] = cs * LOG2E
        rtT_ref[c] = ((cs - jnp.log(dt)) * LOG2E).T


def _dtprep(dt_raw, bias_row, alog_row):
    m = dt_raw.shape[0]
    rows = DTPREP_CHUNKS * CHUNK
    return pl.pallas_call(
        _dtprep_kernel,
        grid=(m // rows,),
        in_specs=[pl.BlockSpec((rows, LANES), lambda i: (i, 0)),
                  _resident((1, LANES)),
                  _resident((1, LANES))],
        out_specs=[pl.BlockSpec((rows, LANES), lambda i: (i, 0)),
                   pl.BlockSpec((DTPREP_CHUNKS, CHUNK, CHUNK), lambda i: (i, 0, 0))],
        out_shape=[jax.ShapeDtypeStruct((m, LANES), F32),
                   jax.ShapeDtypeStruct((m // CHUNK, CHUNK, CHUNK), F32)],
        compiler_params=_cparams(("arbitrary",)),
        name="dtprep",
    )(dt_raw, bias_row, alog_row)


def _half_norm(x, g, half_ones):
    ss = x * x
    hi = ss.astype(BF16)
    lo = (ss - hi.astype(F32)).astype(BF16)
    ms = _dot(jnp.concatenate([hi, lo], axis=1), half_ones) * (1.0 / ATTN_DH)
    return x * lax.rsqrt(ms + EPS) * g


def _rope(x, cos, sin_up, sin_dn):
    x_up = pltpu.roll(x, HEAD_W - ROPE_HALF, axis=1)
    x_dn = pltpu.roll(x, ROPE_HALF, axis=1)
    return x * cos + x_up * sin_up + x_dn * sin_dn


def _attn_kernel(*refs, lq, lctx, hps, lam_init, rope, emit_kv):
    it = iter(refs)
    lamp_ref, qg_ref, kg_ref, sg_ref, q_ref, k_ref, v_ref = (next(it) for _ in range(7))
    if rope:
        cos_ref, sup_ref, sdn_ref = next(it), next(it), next(it)
    if lctx:
        ckT_ref, cv_ref = next(it), next(it)
    o_ref = next(it)
    if emit_kv:
        knT_ref, vo_ref = next(it), next(it)
    head_scr = [tuple(next(it) for _ in range(4)) for _ in range(hps)]
    if lctx:
        kcT_scrs = [next(it) for _ in range(hps)]

    lp = lamp_ref[...]
    lam = (jnp.exp(jnp.sum(lp[0:1] * lp[1:2], axis=-1, keepdims=True))
           - jnp.exp(jnp.sum(lp[2:3] * lp[3:4], axis=-1, keepdims=True)) + lam_init)

    r_i = (lax.broadcasted_iota(jnp.int32, (2 * HEAD_W, HEAD_W), 0) // ATTN_DH) & 1
    c_i = lax.broadcasted_iota(jnp.int32, (2 * HEAD_W, HEAD_W), 1) // ATTN_DH
    half_ones = jnp.where(r_i == c_i, 1.0, 0.0).astype(BF16)

    first_half = lax.broadcasted_iota(jnp.int32, (lq, HEAD_W), 1) < ATTN_DH
    sub_gain = sg_ref[...] * (1.0 - lam_init)
    ones = jnp.ones((lq + lctx, ATTN_VD), BF16)

    def stage_ctx_values(va_scr, hh):
        head = pl.program_id(1) * hps + hh
        rows = cv_ref[0, pl.ds(head, lctx, stride=ATTN_HEADS), :]
        va_scr[lq:lq + lctx, 0:ATTN_VD] = rows.astype(BF16)

    for hh in range(hps):
        cols = slice(hh * HEAD_W, (hh + 1) * HEAD_W)
        qn = _half_norm(q_ref[:, cols].astype(F32), qg_ref[...], half_ones)
        kn = _half_norm(k_ref[:, cols].astype(F32), kg_ref[...], half_ones)
        v = v_ref[:, cols]
        if emit_kv:
            knT_ref[0, hh] = kn.T
            vo_ref[:, cols] = v.astype(F32)
        if rope:
            qn = _rope(qn, cos_ref[...], sup_ref[...], sdn_ref[...])
            kn = _rope(kn, cos_ref[...], sup_ref[...], sdn_ref[...])
        qs = qn * (ATTN_DH ** -0.5 * LOG2E)
        q1_scr, q2_scr, kk_scr, va_scr = head_scr[hh]
        q1_scr[...] = jnp.where(first_half, qs, 0.0).astype(BF16)
        q2_scr[...] = jnp.where(first_half, 0.0, qs).astype(BF16)
        kk_scr[...] = kn.astype(BF16)
        va_scr[0:lq, 0:ATTN_VD] = v.astype(BF16)
        va_scr[:, ATTN_VD:2 * ATTN_VD] = ones
        if lctx:
            kcT_scrs[hh][...] = ckT_ref[0, hh].astype(BF16)
            stage_ctx_values(va_scr, hh)

        def softmax_pv(qb, kk_scr=kk_scr, va_scr=va_scr, hh=hh):
            s = _dot_nt(qb, kk_scr[...])
            if lctx:
                s = jnp.concatenate([s, _dot(qb, kcT_scrs[hh][...])], axis=1)
            p = jnp.exp2(s - jnp.max(s, axis=-1, keepdims=True)).astype(BF16)
            a = _dot(p, va_scr[...])
            return a[:, 0:ATTN_VD] / a[:, ATTN_VD:2 * ATTN_VD]

        o = softmax_pv(q1_scr[...]) - lam * softmax_pv(q2_scr[...])
        o = o * lax.rsqrt(jnp.mean(o * o, axis=-1, keepdims=True) + EPS) * sub_gain
        o_ref[:, cols] = o.astype(o_ref.dtype)


def _attn(p2d, lamp, qg, kg, sg, batch, lq, lam_init, hps, rope_tabs=None, ctx=None, emit_kv=False):
    m = p2d.shape[0]
    lctx = 0 if ctx is None else ctx[0].shape[-1]
    width = hps * HEAD_W
    qb, kb, vb = P_Q // width, P_K // width, P_V // width
    const = lambda b, h: (0, 0)
    in_specs = [pl.BlockSpec((4, ATTN_DH), const),
                pl.BlockSpec((1, HEAD_W), const),
                pl.BlockSpec((1, HEAD_W), const),
                pl.BlockSpec((1, ATTN_VD), const),
                pl.BlockSpec((lq, width), lambda b, h: (b, qb + h)),
                pl.BlockSpec((lq, width), lambda b, h: (b, kb + h)),
                pl.BlockSpec((lq, width), lambda b, h: (b, vb + h))]
    args = [lamp, qg, kg, sg, p2d, p2d, p2d]
    if rope_tabs is not None:
        in_specs += [pl.BlockSpec((lq, HEAD_W), const)] * 3
        args += list(rope_tabs)
    if ctx is not None:
        in_specs += [pl.BlockSpec((1, hps, HEAD_W, lctx), lambda b, h: (b, h, 0, 0)),
                     pl.BlockSpec((1, lctx * ATTN_HEADS, ATTN_VD), lambda b, h: (b, 0, 0))]
        args += list(ctx)
    head_out = pl.BlockSpec((lq, width), lambda b, h: (b, h))
    out_specs = [head_out]
    out_shape = [jax.ShapeDtypeStruct((m, D_MODEL), ACT_DTYPE)]
    if emit_kv:
        out_specs += [pl.BlockSpec((1, hps, HEAD_W, lq), lambda b, h: (b, h, 0, 0)), head_out]
        out_shape += [jax.ShapeDtypeStruct((batch, ATTN_HEADS, HEAD_W, lq), F32),
                      jax.ShapeDtypeStruct((m, D_MODEL), F32)]
    scratch = hps * [pltpu.VMEM((lq, HEAD_W), BF16),
                     pltpu.VMEM((lq, HEAD_W), BF16),
                     pltpu.VMEM((lq, HEAD_W), BF16),
                     pltpu.VMEM((lq + lctx, 2 * ATTN_VD), BF16)]
    if lctx:
        scratch += hps * [pltpu.VMEM((HEAD_W, lctx), BF16)]
    kern = functools.partial(_attn_kernel, lq=lq, lctx=lctx, hps=hps, lam_init=lam_init,
                             rope=rope_tabs is not None, emit_kv=emit_kv)
    return pl.pallas_call(
        kern,
        grid=(batch, ATTN_HEADS // hps),
        in_specs=in_specs,
        out_specs=out_specs,
        out_shape=out_shape,
        scratch_shapes=scratch,
        compiler_params=_cparams(("arbitrary", "arbitrary")),
        name="attn",
    )(*args)


def _ssd_kernel(*refs, seq, has_h0, gps):
    it = iter(refs)
    x_ref, b_ref, c_ref = next(it), next(it), next(it)
    cwx_ref, cwb_ref, cwc_ref = next(it), next(it), next(it)
    cbx_ref, cbb_ref, cbc_ref = next(it), next(it), next(it)
    dexp_ref, cs_ref, rtT_ref = next(it), next(it), next(it)
    if has_h0:
        h0f_ref, h0b_ref = next(it), next(it)
    y_ref, hf_ref, hb_ref = next(it), next(it), next(it)
    group_scr = [tuple(next(it) for _ in range(10)) for _ in range(gps)]

    g0 = pl.program_id(1) * gps
    nc = seq // CHUNK
    x_lanes = [slice(gi * GROUP_W, (gi + 1) * GROUP_W) for gi in range(gps)]
    n_lanes = [slice(gi * SSM_STATE, (gi + 1) * SSM_STATE) for gi in range(gps)]
    blk = lax.broadcasted_iota(jnp.int32, (CHUNK, GROUP_W), 1) // SSM_HEADDIM

    win_rows = CHUNK + 2 * CONV_EDGE
    sel_r = lax.broadcasted_iota(jnp.int32, (CHUNK, win_rows), 0)
    sel_c = lax.broadcasted_iota(jnp.int32, (CHUNK, win_rows), 1)
    taps = [d for d in range(-(D_CONV // 2), D_CONV // 2 + 1) if d != 0]
    shift_sel = {d: jnp.where(sel_c == sel_r + CONV_EDGE + d, 1.0, 0.0).astype(BF16) for d in taps}

    def conv_chunk(c, sources, shift_on_mxu):
        r0 = pl.multiple_of(c * CHUNK, CHUNK)
        p0 = pl.multiple_of(jnp.maximum(r0 - CONV_EDGE, 0), CONV_EDGE)
        n0 = pl.multiple_of(jnp.minimum(r0 + CHUNK, seq - CONV_EDGE), CONV_EDGE)

        def gather(rows_of):
            parts = [rows_of(*src) for src in sources]
            return parts[0] if len(parts) == 1 else jnp.concatenate(parts, axis=1)

        assert all(src[0].dtype == BF16 for src in sources)
        prev = gather(lambda ref, w_ref, b_ref, lanes: ref[pl.ds(p0, CONV_EDGE), lanes])
        nxt = gather(lambda ref, w_ref, b_ref, lanes: ref[pl.ds(n0, CONV_EDGE), lanes])
        main = gather(lambda ref, w_ref, b_ref, lanes: ref[pl.ds(r0, CHUNK), lanes])
        w = [gather(lambda ref, w_ref, b_ref, lanes, j=j: w_ref[j:j + 1, lanes]) for j in range(D_CONV)]
        bias = gather(lambda ref, w_ref, b_ref, lanes: b_ref[:, lanes])
        acc = bias + main.astype(F32) * w[D_CONV // 2]
        if shift_on_mxu:
            zero = jnp.zeros_like(prev)
            win = jnp.concatenate([jnp.where(c > 0, prev, zero), main,
                                   jnp.where(c < nc - 1, nxt, zero)], axis=0)
            for d in taps:
                acc = acc + _dot(shift_sel[d], win) * w[d + D_CONV // 2]
        else:
            rows = CHUNK + 2 * CONV_PAD
            win = jnp.concatenate(
                [jnp.where(c > 0, prev.astype(F32)[CONV_EDGE - CONV_PAD:, :], 0.0), main.astype(F32),
                 jnp.where(c < nc - 1, nxt.astype(F32)[:CONV_PAD, :], 0.0)], axis=0)
            for d in taps:
                tap = pltpu.roll(win, (-d) % rows, axis=0)[CONV_PAD:CONV_PAD + CHUNK, :]
                acc = acc + tap * w[d + D_CONV // 2]
        return _silu(acc)

    head_masks = [jnp.where(blk == j, 1.0, 0.0).astype(BF16) for j in range(HEADS_PER_GROUP)]

    def conv_body(c, carry):
        sl = pl.ds(pl.multiple_of(c * CHUNK, CHUNK), CHUNK)
        for gi in range(gps):
            xm_s, bT_s, cc_s, _, _, _, yacc, g_s, _, _ = group_scr[gi]
            xv = conv_chunk(c, [(x_ref, cwx_ref, cbx_ref, x_lanes[gi])], nc > 2)
            yacc[sl, :] = dexp_ref[:, x_lanes[gi]] * xv
            xb = xv.astype(BF16)
            for j in range(HEADS_PER_GROUP):
                xm_s[c, j * CHUNK:(j + 1) * CHUNK, :] = xb * head_masks[j]
            bc = conv_chunk(c, [(b_ref, cwb_ref, cbb_ref, n_lanes[gi]),
                                (c_ref, cwc_ref, cbc_ref, n_lanes[gi])], False)
            bT = bc[:, 0:SSM_STATE].T.astype(BF16)
            cc = bc[:, SSM_STATE:2 * SSM_STATE].astype(BF16)
            bT_s[c] = bT
            cc_s[sl, :] = cc
            g_s[c] = _dot(cc, bT)
        return carry

    lax.fori_loop(0, nc, conv_body, 0, unroll=2)

    for gi in range(gps):
        csg_s, stf, stb = group_scr[gi][3:6]
        shift = (LANES - HEADS_PER_GROUP * (g0 + gi)) % LANES
        csg_s[...] = pltpu.roll(cs_ref[...], shift, axis=1)
        if has_h0:
            stf[...] = h0f_ref[0, x_lanes[gi], :].T
            stb[...] = h0b_ref[0, x_lanes[gi], :].T
        else:
            stf[...] = jnp.zeros_like(stf)
            stb[...] = jnp.zeros_like(stb)

    row = lax.broadcasted_iota(jnp.int32, (CHUNK, CHUNK), 0)
    col = lax.broadcasted_iota(jnp.int32, (CHUNK, CHUNK), 1)
    lane_lo = lax.broadcasted_iota(jnp.int32, (CHUNK, LANES), 1) < SSM_HEADDIM

    def decay_terms(c, gi):
        xm_s, bT_s, cc_s, csg_s, _, _, yacc, g_s, upd_s, e_s = group_scr[gi]
        g = g0 + gi
        sl = pl.ds(pl.multiple_of(c * CHUNK, CHUNK), CHUNK)
        bT = bT_s[c]
        cs = csg_s[sl, :]
        x_heads = xm_s[c]
        dec_sum = [None] * HEADS_PER_GROUP
        for d, (lane0, mask, edge) in enumerate(((0, col <= row, CHUNK - 1), (SSM_HEADS, col >= row, 0))):
            upd_lhs, cols = [], []
            for j in range(HEADS_PER_GROUP):
                cb = jnp.broadcast_to(cs[:, lane0 + j:lane0 + j + 1], (CHUNK, CHUNK))
                rrow = rtT_ref[c, pl.ds(lane0 + HEADS_PER_GROUP * g + j, 1), :]
                dec = jnp.exp2(jnp.where(mask, cb - rrow, -jnp.inf))
                dec_sum[j] = dec if dec_sum[j] is None else dec_sum[j] + dec
                w_edge = dec[edge:edge + 1, :]
                upd_lhs.append(bT * w_edge.astype(BF16))
                cols.append(cb)
            upd_s[d, c] = _dot(jnp.concatenate(upd_lhs, axis=1), x_heads)
            cs_e = jnp.concatenate([jnp.where(lane_lo, cols[2 * t], cols[2 * t + 1])
                                    for t in range(HEADS_PER_GROUP // 2)], axis=1)
            e_s[d, c] = jnp.exp2(cs_e)
        gmat = g_s[c]
        diag_lhs = [(gmat * dec_sum[j]).astype(BF16) for j in range(HEADS_PER_GROUP)]
        yacc[sl, :] = yacc[sl, :] + _dot(jnp.concatenate(diag_lhs, axis=1), x_heads)

    def decay_body(c, carry):
        for gi in range(gps):
            decay_terms(c, gi)
        return carry

    lax.fori_loop(0, nc, decay_body, 0, unroll=2)

    def state_step(c, d, gi):
        _, _, cc_s, _, stf, stb, yacc, _, upd_s, e_s = group_scr[gi]
        st_ref = stb if d else stf
        edge = 0 if d else CHUNK - 1
        sl = pl.ds(pl.multiple_of(c * CHUNK, CHUNK), CHUNK)
        e = e_s[d, c]
        st = st_ref[...]
        yacc[sl, :] = yacc[sl, :] + e * _dot(cc_s[sl, :], st.astype(BF16))
        st_ref[...] = e[edge:edge + 1, :] * st + upd_s[d, c]

    def state_body(i, carry):
        for gi in range(gps):
            state_step(i, 0, gi)
            state_step(nc - 1 - i, 1, gi)
        return carry

    lax.fori_loop(0, nc, state_body, 0, unroll=2)

    for gi in range(gps):
        stf, stb, yacc = group_scr[gi][4:7]
        y_ref[:, x_lanes[gi]] = yacc[...].astype(y_ref.dtype)
        hf_ref[0, x_lanes[gi], :] = stf[...].T
        hb_ref[0, x_lanes[gi], :] = stb[...].T


def _ssd(p2d, cs, rtT, conv_w, conv_b, dexp, batch, seq, gps, h0=None):
    m = p2d.shape[0]
    nc = seq // CHUNK
    xw, nw = gps * GROUP_W, gps * SSM_STATE
    xb, bb, cb = P_X // xw, P_B // nw, P_C // nw
    cw_b0 = D_INNER // nw
    cw_c0 = cw_b0 + SSM_GROUPS // gps
    in_specs = [pl.BlockSpec((seq, xw), lambda b, g: (b, xb + g)),
                pl.BlockSpec((seq, nw), lambda b, g: (b, bb + g)),
                pl.BlockSpec((seq, nw), lambda b, g: (b, cb + g)),
                pl.BlockSpec((D_CONV, xw), lambda b, g: (0, g)),
                pl.BlockSpec((D_CONV, nw), lambda b, g: (0, cw_b0 + g)),
                pl.BlockSpec((D_CONV, nw), lambda b, g: (0, cw_c0 + g)),
                pl.BlockSpec((1, xw), lambda b, g: (0, g)),
                pl.BlockSpec((1, nw), lambda b, g: (0, cw_b0 + g)),
                pl.BlockSpec((1, nw), lambda b, g: (0, cw_c0 + g)),
                pl.BlockSpec((1, xw), lambda b, g: (0, g)),
                pl.BlockSpec((seq, LANES), lambda b, g: (b, 0)),
                pl.BlockSpec((nc, CHUNK, CHUNK), lambda b, g: (b, 0, 0))]
    args = [p2d, p2d, p2d, conv_w, conv_w, conv_w, conv_b, conv_b, conv_b, dexp, cs, rtT]
    state_spec = pl.BlockSpec((1, xw, SSM_STATE), lambda b, g: (b, g, 0))
    if h0 is not None:
        in_specs += [state_spec, state_spec]
        args += list(h0)
    state_shape = jax.ShapeDtypeStruct((batch, D_INNER, SSM_STATE), F32)
    kern = functools.partial(_ssd_kernel, seq=seq, has_h0=h0 is not None, gps=gps)
    return pl.pallas_call(
        kern,
        grid=(batch, SSM_GROUPS // gps),
        in_specs=in_specs,
        out_specs=[pl.BlockSpec((seq, xw), lambda b, g: (b, g)), state_spec, state_spec],
        out_shape=[jax.ShapeDtypeStruct((m, D_INNER), ACT_DTYPE), state_shape, state_shape],
        scratch_shapes=gps * [pltpu.VMEM((nc, HEADS_PER_GROUP * CHUNK, GROUP_W), BF16),
                              pltpu.VMEM((nc, SSM_STATE, CHUNK), BF16),
                              pltpu.VMEM((seq, SSM_STATE), BF16),
                              pltpu.VMEM((seq, LANES), F32),
                              pltpu.VMEM((SSM_STATE, GROUP_W), F32),
                              pltpu.VMEM((SSM_STATE, GROUP_W), F32),
                              pltpu.VMEM((seq, GROUP_W), F32),
                              pltpu.VMEM((nc, CHUNK, CHUNK), F32),
                              pltpu.VMEM((2, nc, SSM_STATE, GROUP_W), F32),
                              pltpu.VMEM((2, nc, CHUNK, GROUP_W), F32)],
        compiler_params=_cparams(("arbitrary", "arbitrary")),
        name="ssd",
    )(*args)


def _merge_kernel(x_ref, mod_ref, o_ref, y_ref, z_ref, ga_ref, gb_ref, sg_ref,
                  wa_ref, wb_ref, wo_ref, out_ref):
    out_a = _dot(o_ref[...], wa_ref[...])
    yz = y_ref[...].astype(F32) * _silu(z_ref[...].astype(F32))
    yn = yz * lax.rsqrt(jnp.mean(yz * yz, axis=-1, keepdims=True) + EPS) * sg_ref[...]
    out_b = _dot(yn.astype(BF16), wb_ref[...])
    merged = (jax.nn.sigmoid(ga_ref[...].astype(F32)) * out_a
              + jax.nn.sigmoid(gb_ref[...].astype(F32)) * out_b)
    gate1 = mod_ref[0, :, 2 * D_MODEL:3 * D_MODEL]
    out_ref[...] = x_ref[...] + gate1 * _dot(merged.astype(BF16), wo_ref[...])


def _merge(x2d, mod, o, y, p2d, ssm_g, wa, wb, wo, tm):
    m = x2d.shape[0]
    rows_per_mod = m // mod.shape[0]
    return pl.pallas_call(
        _merge_kernel,
        grid=(m // tm,),
        in_specs=[pl.BlockSpec((tm, D_MODEL), lambda i: (i, 0)),
                  pl.BlockSpec((1, 1, 6 * D_MODEL), lambda i: ((i * tm) // rows_per_mod, 0, 0)),
                  pl.BlockSpec((tm, D_MODEL), lambda i: (i, 0)),
                  pl.BlockSpec((tm, D_INNER), lambda i: (i, 0)),
                  pl.BlockSpec((tm, D_INNER), lambda i: (i, P_Z // D_INNER)),
                  pl.BlockSpec((tm, D_MODEL), lambda i: (i, P_GA // D_MODEL)),
                  pl.BlockSpec((tm, D_MODEL), lambda i: (i, P_GB // D_MODEL)),
                  _resident((1, D_INNER)),
                  _resident((D_MODEL, D_MODEL)),
                  _resident((D_INNER, D_MODEL)),
                  _resident((D_MODEL, D_MODEL))],
        out_specs=pl.BlockSpec((tm, D_MODEL), lambda i: (i, 0)),
        out_shape=jax.ShapeDtypeStruct((m, D_MODEL), F32),
        compiler_params=_cparams(("arbitrary",)),
        name="merge",
    )(x2d, mod, o, y, p2d, p2d, p2d, ssm_g, wa, wb, wo)


def _ffn_kernel(x_ref, mod_ref, g_ref, wg_ref, wu_ref, wd_ref, out_ref):
    x = x_ref[...]
    y = x * lax.rsqrt(jnp.mean(x * x, axis=-1, keepdims=True) + EPS) * g_ref[...]
    shift = mod_ref[0, :, 3 * D_MODEL:4 * D_MODEL]
    scale = mod_ref[0, :, 4 * D_MODEL:5 * D_MODEL]
    gate = mod_ref[0, :, 5 * D_MODEL:6 * D_MODEL]
    h = (y * (1.0 + scale) + shift).astype(BF16)
    f = (_silu(_dot(h, wg_ref[...])) * _dot(h, wu_ref[...])).astype(BF16)
    out_ref[...] = x + gate * _dot(f, wd_ref[...])


def _ffn(x2d, mod, g2, wg, wu, wd, tm):
    m = x2d.shape[0]
    rows_per_mod = m // mod.shape[0]
    return pl.pallas_call(
        _ffn_kernel,
        grid=(m // tm,),
        in_specs=[pl.BlockSpec((tm, D_MODEL), lambda i: (i, 0)),
                  pl.BlockSpec((1, 1, 6 * D_MODEL), lambda i: ((i * tm) // rows_per_mod, 0, 0)),
                  _resident((1, D_MODEL)),
                  _resident((D_MODEL, D_FF)),
                  _resident((D_MODEL, D_FF)),
                  _resident((D_FF, D_MODEL))],
        out_specs=pl.BlockSpec((tm, D_MODEL), lambda i: (i, 0)),
        out_shape=jax.ShapeDtypeStruct((m, D_MODEL), F32),
        compiler_params=_cparams(("arbitrary",)),
        name="ffn",
    )(x2d, mod, g2, wg, wu, wd)


NORM_ROWS = 1024
PROJ_ROWS = 4096
MIX_ROWS = 512
ATTN_STEP_ROWS = 2048
ATTN_STEP_ROWS_CTX = 4096


def _tiles(rows, n_mod, seq, has_ctx):
    rows_per_mod = rows // n_mod
    attn_rows = ATTN_STEP_ROWS_CTX if has_ctx else ATTN_STEP_ROWS
    return dict(
        norm_rows=min(NORM_ROWS, rows_per_mod),
        proj_rows=min(PROJ_ROWS, rows),
        mix_rows=min(MIX_ROWS, rows_per_mod),
        heads_per_step=max(1, min(ATTN_HEADS, attn_rows // seq)),
        groups_per_step=SSD_GROUPS_PER_STEP,
    )


def _rope_tables(n_tok):
    pos = jnp.arange(n_tok, dtype=jnp.int32)
    row = (pos // GRID_W).astype(F32)
    colp = (pos % GRID_W).astype(F32)
    inv = ROPE_BASE ** (-jnp.arange(0, ROPE_AXIS_DIM, 2, dtype=F32) / ROPE_AXIS_DIM)
    ang_r = row[:, None] * inv[None, :]
    ang_c = colp[:, None] * inv[None, :]
    zero = jnp.zeros_like(ang_r)
    cos = jnp.concatenate([jnp.cos(ang_r)] * 2 + [jnp.cos(ang_c)] * 2, axis=-1)
    sup = jnp.concatenate([-jnp.sin(ang_r), zero, -jnp.sin(ang_c), zero], axis=-1)
    sdn = jnp.concatenate([zero, jnp.sin(ang_r), zero, jnp.sin(ang_c)], axis=-1)
    return tuple(jnp.tile(t, (1, 2)) for t in (cos, sup, sdn))


def _pad_lanes(v):
    flat = v.reshape(1, -1).astype(F32)
    return jnp.pad(flat, ((0, 0), (0, LANES - flat.shape[1])))


def kernel(x_prompt, x_sample, c, cache_k, cache_v, state_ssm_fwd, state_ssm_bwd, c_ctx, norm1_g, norm2_g, w_ada, b_ada, w_in, q_norm_g, k_norm_g, lambda_q1, lambda_k1, lambda_q2, lambda_k2, attn_sub_g, conv_w, conv_b, A_log, dt_bias, D_skip, ssm_norm_g, w_branch_a, w_branch_b, w_out, w_ffn_gate, w_ffn_up, w_ffn_down):
    depth = norm1_g.shape[0]
    assert depth == 1, "single-layer kernel"
    l = 0
    lam_init = 0.8 - 0.6 * math.exp(-0.3 * l)
    nb, seq, _ = x_prompt.shape
    db, dseq, _ = x_sample.shape

    cond = jnp.concatenate([c_ctx[None, :], c, jnp.zeros((SUBLANES - 1 - db, D_MODEL), F32)], axis=0)
    mod = _ada(cond, w_ada[l], b_ada[l][None, :])
    mod_ctx = mod[0:1].reshape(1, 1, 6 * D_MODEL)
    mod_lat = mod[1:1 + db].reshape(db, 1, 6 * D_MODEL)

    w = jnp.transpose(w_in[l])
    w_gates = w[W_G0:W_G0 + 2 * D_MODEL]
    w_dt = jnp.pad(w[W_DT0:W_G0], ((0, LANES - 2 * SSM_HEADS), (0, 0)))
    wa = w_branch_a[l].astype(BF16)
    wb = w_branch_b[l].astype(BF16)
    wo = w_out[l].astype(BF16)
    wg = w_ffn_gate[l].astype(BF16)
    wu = w_ffn_up[l].astype(BF16)
    wd = w_ffn_down[l].astype(BF16)
    g1 = norm1_g[l][None, :]
    g2 = norm2_g[l][None, :]
    qg = jnp.tile(q_norm_g[l], 2)[None, :]
    kg = jnp.tile(k_norm_g[l], 2)[None, :]
    sg = attn_sub_g[l][None, :]
    lamp = jnp.stack([lambda_q1[l], lambda_k1[l], lambda_q2[l], lambda_k2[l]], axis=0)
    bias_row = _pad_lanes(dt_bias[l])
    alog_row = _pad_lanes(A_log[l])
    dexp = jnp.repeat(D_skip[l], SSM_HEADDIM)[None, :]
    cw = conv_w[l]
    cb = conv_b[l][None, :]
    ssm_g = ssm_norm_g[l][None, :]

    def layer(x, mod_rows, rope_tabs, ctx, h0, emit_kv):
        batch, sl, _ = x.shape
        x2d = x.reshape(batch * sl, D_MODEL)
        tiles = _tiles(batch * sl, mod_rows.shape[0], sl, ctx is not None)
        h, dt_raw = _norm1(x2d, mod_rows, g1, w_dt, tm=tiles["norm_rows"])
        p2d = _proj(h, w, w_gates, tm=tiles["proj_rows"])
        cs, rtT = _dtprep(dt_raw, bias_row, alog_row)
        attn_out = _attn(p2d, lamp, qg, kg, sg, batch, sl, lam_init, tiles["heads_per_step"],
                         rope_tabs, ctx, emit_kv)
        y, hf, hb = _ssd(p2d, cs, rtT, cw, cb, dexp, batch, sl, tiles["groups_per_step"], h0)
        x1 = _merge(x2d, mod_rows, attn_out[0], y, p2d, ssm_g, wa, wb, wo, tm=tiles["mix_rows"])
        x2 = _ffn(x1, mod_rows, g2, wg, wu, wd, tm=tiles["mix_rows"])
        return x2.reshape(batch, sl, D_MODEL), attn_out[1:], hf, hb

    y_prompt, kv, hf, hb = layer(x_prompt, mod_ctx, None, None, None, True)
    ctx_kT = jnp.transpose(cache_k[:, l], (0, 2, 3, 4, 1)).reshape(db, ATTN_HEADS, HEAD_W, -1)
    ctx_v = cache_v[:, l].reshape(db, -1, ATTN_VD)
    h0 = (state_ssm_fwd[:, l].reshape(db, D_INNER, SSM_STATE),
          state_ssm_bwd[:, l].reshape(db, D_INNER, SSM_STATE))
    y_sample, _, _, _ = layer(x_sample, mod_lat, _rope_tables(dseq), (ctx_kT, ctx_v), h0, False)

    knT = kv[0].reshape(nb, ATTN_HEADS, 2, ATTN_DH, seq)
    new_cache_k = jnp.transpose(knT, (0, 4, 1, 2, 3)).reshape(nb, 1, seq, ATTN_HEADS, 2, ATTN_DH)
    new_cache_v = kv[1].reshape(nb, 1, seq, ATTN_HEADS, ATTN_VD)
    new_hf = hf.reshape(nb, 1, SSM_HEADS, SSM_HEADDIM, SSM_STATE)
    new_hb = hb.reshape(nb, 1, SSM_HEADS, SSM_HEADDIM, SSM_STATE)
    return (y_prompt, y_sample, new_cache_k, new_cache_v, new_hf, new_hb)
```

```python
import functools
import math

import numpy as np
import jax
import jax.numpy as jnp
from jax import lax
from jax.experimental import pallas as pl
from jax.experimental.pallas import tpu as pltpu

F32 = jnp.float32
BF16 = jnp.bfloat16

D_MODEL = 1024
GRID_W = 64
ATTN_HEADS = 8
ATTN_DH = 64
ATTN_VD = 128
HEAD_W = 2 * ATTN_DH
ROPE_BASE = 10000.0
ROPE_AXIS_DIM = ATTN_DH // 2
ROPE_HALF = ROPE_AXIS_DIM // 2
D_INNER = 2 * D_MODEL
SSM_HEADDIM = 64
SSM_HEADS = D_INNER // SSM_HEADDIM
SSM_GROUPS = 8
HEADS_PER_GROUP = SSM_HEADS // SSM_GROUPS
GROUP_W = HEADS_PER_GROUP * SSM_HEADDIM
SSM_STATE = 128
D_CONV = 5
CONV_PAD = 8
CONV_EDGE = 16
CHUNK = 128
D_FF = ((8 * D_MODEL // 3 + 255) // 256) * 256
EPS = 1e-6
LANES = 128
SUBLANES = 8
LOG2E = math.log2(math.e)

P_Z = 0
P_GA = P_Z + D_INNER
P_GB = P_GA + D_MODEL
P_X = P_GB + D_MODEL
P_B = P_X + D_INNER
P_C = P_B + SSM_GROUPS * SSM_STATE
P_Q = P_C + SSM_GROUPS * SSM_STATE
P_K = P_Q + D_MODEL
P_V = P_K + D_MODEL
NP = P_V + D_MODEL
PROJ_TN = 512
ACT_DTYPE = BF16

W_Q0 = 0
W_Z0 = 3 * D_MODEL
W_XBC0 = W_Z0 + D_INNER
W_DT0 = W_XBC0 + D_INNER + 2 * SSM_GROUPS * SSM_STATE
W_G0 = W_DT0 + 2 * SSM_HEADS

VMEM_LIMIT = 56 * 1024 * 1024


def _cparams(sem):
    return pltpu.CompilerParams(dimension_semantics=sem, vmem_limit_bytes=VMEM_LIMIT)


def _silu(x):
    return x * jax.nn.sigmoid(x)


def _dot(a, b):
    return jnp.dot(a, b, preferred_element_type=F32)


def _dot_nt(a, b):
    return lax.dot_general(a, b, (((1,), (1,)), ((), ())), preferred_element_type=F32)


def _split3(x):
    hi = x.astype(BF16)
    r = x - hi.astype(F32)
    mid = r.astype(BF16)
    lo = (r - mid.astype(F32)).astype(BF16)
    return hi, mid, lo


def _resident(shape):
    return pl.BlockSpec(shape, lambda *_: (0,) * len(shape), pipeline_mode=pl.Buffered(1))


def _ada_kernel(cond_ref, w_ref, b_ref, o_ref):
    s = _silu(cond_ref[...]).astype(BF16)
    o_ref[...] = _dot(s, w_ref[...].astype(BF16)) + b_ref[...]


def _ada(cond, w_ada, b_ada):
    rows = cond.shape[0]
    n = w_ada.shape[1]
    tn = 1024
    return pl.pallas_call(
        _ada_kernel,
        grid=(n // tn,),
        in_specs=[pl.BlockSpec((rows, D_MODEL), lambda j: (0, 0)),
                  pl.BlockSpec((D_MODEL, tn), lambda j: (0, j)),
                  pl.BlockSpec((1, tn), lambda j: (0, j))],
        out_specs=pl.BlockSpec((rows, tn), lambda j: (0, j)),
        out_shape=jax.ShapeDtypeStruct((rows, n), F32),
        compiler_params=_cparams(("arbitrary",)),
        name="ada",
    )(cond, w_ada, b_ada)


def _norm1_kernel(x_ref, mod_ref, g_ref, wdt_ref, h_ref, dt_ref):
    x = x_ref[...]
    y = x * lax.rsqrt(jnp.mean(x * x, axis=-1, keepdims=True) + EPS) * g_ref[...]
    shift = mod_ref[0, :, 0:D_MODEL]
    scale = mod_ref[0, :, D_MODEL:2 * D_MODEL]
    h = (y * (1.0 + scale) + shift).astype(BF16)
    h_ref[...] = h
    dt_ref[...] = _dot_nt(h, wdt_ref[...].astype(BF16))


def _norm1(x2d, mod, g1, w_dt, tm):
    m = x2d.shape[0]
    rows_per_mod = m // mod.shape[0]
    return pl.pallas_call(
        _norm1_kernel,
        grid=(m // tm,),
        in_specs=[pl.BlockSpec((tm, D_MODEL), lambda i: (i, 0)),
                  pl.BlockSpec((1, 1, 6 * D_MODEL), lambda i: ((i * tm) // rows_per_mod, 0, 0)),
                  _resident((1, D_MODEL)),
                  _resident((LANES, D_MODEL))],
        out_specs=[pl.BlockSpec((tm, D_MODEL), lambda i: (i, 0)),
                   pl.BlockSpec((tm, LANES), lambda i: (i, 0))],
        out_shape=[jax.ShapeDtypeStruct((m, D_MODEL), BF16),
                   jax.ShapeDtypeStruct((m, LANES), F32)],
        compiler_params=_cparams(("arbitrary",)),
        name="norm1",
    )(x2d, mod, g1, w_dt)


_GATE_TILE0 = P_GA // PROJ_TN
_N_GATE_TILES = 2 * D_MODEL // PROJ_TN


def _w_in_tile(j):
    z_tiles = D_INNER // PROJ_TN
    xbc_tile0 = (P_X // PROJ_TN)
    qkv_tile0 = (P_Q // PROJ_TN)
    z_src = jnp.minimum(j, z_tiles - 1) + W_Z0 // PROJ_TN
    xbc_src = j - xbc_tile0 + W_XBC0 // PROJ_TN
    qkv_src = j - qkv_tile0 + W_Q0 // PROJ_TN
    return jnp.where(j < xbc_tile0, z_src, jnp.where(j < qkv_tile0, xbc_src, qkv_src))


def _proj_kernel(h_ref, win_ref, wg_ref, p_ref, w_scr, *, tm):
    j = pl.program_id(0)
    i = pl.program_id(1)
    is_gate = jnp.logical_and(j >= _GATE_TILE0, j < _GATE_TILE0 + _N_GATE_TILES)

    @pl.when(jnp.logical_and(i == 0, is_gate))
    def _():
        w_scr[...] = wg_ref[...].astype(BF16)

    @pl.when(jnp.logical_and(i == 0, jnp.logical_not(is_gate)))
    def _():
        w_scr[...] = win_ref[...].astype(BF16)

    r0 = pl.multiple_of(i * tm, tm)
    p_ref[...] = _dot_nt(h_ref[pl.ds(r0, tm), :], w_scr[...]).astype(p_ref.dtype)


def _proj(h, w_in_t, w_gates_t, tm):
    m = h.shape[0]
    tn = PROJ_TN
    return pl.pallas_call(
        functools.partial(_proj_kernel, tm=tm),
        grid=(NP // tn, m // tm),
        in_specs=[_resident((m, D_MODEL)),
                  pl.BlockSpec((tn, D_MODEL), lambda j, i: (_w_in_tile(j), 0)),
                  pl.BlockSpec((tn, D_MODEL),
                               lambda j, i: (jnp.clip(j - _GATE_TILE0, 0, _N_GATE_TILES - 1), 0))],
        out_specs=pl.BlockSpec((tm, tn), lambda j, i: (i, j)),
        out_shape=jax.ShapeDtypeStruct((m, NP), ACT_DTYPE),
        scratch_shapes=[pltpu.VMEM((tn, D_MODEL), BF16)],
        compiler_params=_cparams(("arbitrary", "arbitrary")),
        name="proj",
    )(h, w_in_t, w_gates_t)


DTPREP_CHUNKS = 8
SSD_GROUPS_PER_STEP = 4

def _dtprep_kernel(dtr_ref, bias_ref, alog_ref, cs_ref, rtT_ref):
    row = lax.broadcasted_iota(jnp.int32, (CHUNK, CHUNK), 0)
    col = lax.broadcasted_iota(jnp.int32, (CHUNK, CHUNK), 1)
    tril = jnp.where(col <= row, 1.0, 0.0).astype(BF16)
    triu = jnp.where(col >= row, 1.0, 0.0).astype(BF16)
    neg_a = -jnp.exp(alog_ref[...])
    for c in range(DTPREP_CHUNKS):
        sl = slice(c * CHUNK, (c + 1) * CHUNK)
        x = dtr_ref[sl, :] + bias_ref[...]
        dt = jnp.maximum(x, 0.0) + jnp.log1p(jnp.exp(-jnp.abs(x)))
        hi, mid, lo = _split3(dt * neg_a)
        cf = _dot(tril, hi) + _dot(tril, mid) + _dot(tril, lo)
        cb = _dot(triu, hi) + _dot(triu, mid) + _dot(triu, lo)
        cs = jnp.where(col < SSM_HEADS, cf, cb)
        cs_ref[sl, :] = cs * LOG2E
        rtT_ref[c] = ((cs - jnp.log(dt)) * LOG2E).T


def _dtprep(dt_raw, bias_row, alog_row):
    m = dt_raw.shape[0]
    rows = DTPREP_CHUNKS * CHUNK
    return pl.pallas_call(
        _dtprep_kernel,
        grid=(m // rows,),
        in_specs=[pl.BlockSpec((rows, LANES), lambda i: (i, 0)),
                  _resident((1, LANES)),
                  _resident((1, LANES))],
        out_specs=[pl.BlockSpec((rows, LANES), lambda i: (i, 0)),
                   pl.BlockSpec((DTPREP_CHUNKS, CHUNK, CHUNK), lambda i: (i, 0, 0))],
        out_shape=[jax.ShapeDtypeStruct((m, LANES), F32),
                   jax.ShapeDtypeStruct((m // CHUNK, CHUNK, CHUNK), F32)],
        compiler_params=_cparams(("arbitrary",)),
        name="dtprep",
    )(dt_raw, bias_row, alog_row)


def _half_norm(x, g, half_ones):
    ss = x * x
    hi = ss.astype(BF16)
    lo = (ss - hi.astype(F32)).astype(BF16)
    ms = _dot(jnp.concatenate([hi, lo], axis=1), half_ones) * (1.0 / ATTN_DH)
    return x * lax.rsqrt(ms + EPS) * g


def _rope(x, cos, sin_up, sin_dn):
    x_up = pltpu.roll(x, HEAD_W - ROPE_HALF, axis=1)
    x_dn = pltpu.roll(x, ROPE_HALF, axis=1)
    return x * cos + x_up * sin_up + x_dn * sin_dn


def _attn_kernel(*refs, lq, lctx, hps, lam_init, rope, emit_kv):
    it = iter(refs)
    lamp_ref, qg_ref, kg_ref, sg_ref, q_ref, k_ref, v_ref = (next(it) for _ in range(7))
    if rope:
        cos_ref, sup_ref, sdn_ref = next(it), next(it), next(it)
    if lctx:
        ckT_ref, cv_ref = next(it), next(it)
    o_ref = next(it)
    if emit_kv:
        knT_ref, vo_ref = next(it), next(it)
    head_scr = [tuple(next(it) for _ in range(4)) for _ in range(hps)]
    if lctx:
        kcT_scrs = [next(it) for _ in range(hps)]

    lp = lamp_ref[...]
    lam = (jnp.exp(jnp.sum(lp[0:1] * lp[1:2], axis=-1, keepdims=True))
           - jnp.exp(jnp.sum(lp[2:3] * lp[3:4], axis=-1, keepdims=True)) + lam_init)

    r_i = (lax.broadcasted_iota(jnp.int32, (2 * HEAD_W, HEAD_W), 0) // ATTN_DH) & 1
    c_i = lax.broadcasted_iota(jnp.int32, (2 * HEAD_W, HEAD_W), 1) // ATTN_DH
    half_ones = jnp.where(r_i == c_i, 1.0, 0.0).astype(BF16)

    first_half = lax.broadcasted_iota(jnp.int32, (lq, HEAD_W), 1) < ATTN_DH
    sub_gain = sg_ref[...] * (1.0 - lam_init)
    ones = jnp.ones((lq + lctx, ATTN_VD), BF16)

    def stage_ctx_values(va_scr, hh):
        head = pl.program_id(1) * hps + hh
        rows = cv_ref[0, pl.ds(head, lctx, stride=ATTN_HEADS), :]
        va_scr[lq:lq + lctx, 0:ATTN_VD] = rows.astype(BF16)

    for hh in range(hps):
        cols = slice(hh * HEAD_W, (hh + 1) * HEAD_W)
        qn = _half_norm(q_ref[:, cols].astype(F32), qg_ref[...], half_ones)
        kn = _half_norm(k_ref[:, cols].astype(F32), kg_ref[...], half_ones)
        v = v_ref[:, cols]
        if emit_kv:
            knT_ref[0, hh] = kn.T
            head = pl.program_id(1) * hps + hh
            vo_ref[0, pl.ds(head, lq, stride=ATTN_HEADS), :] = v.astype(F32)
        if rope:
            qn = _rope(qn, cos_ref[...], sup_ref[...], sdn_ref[...])
            kn = _rope(kn, cos_ref[...], sup_ref[...], sdn_ref[...])
        qs = qn * (ATTN_DH ** -0.5 * LOG2E)
        q1_scr, q2_scr, kk_scr, va_scr = head_scr[hh]
        q1_scr[...] = jnp.where(first_half, qs, 0.0).astype(BF16)
        q2_scr[...] = jnp.where(first_half, 0.0, qs).astype(BF16)
        kk_scr[...] = kn.astype(BF16)
        va_scr[0:lq, 0:ATTN_VD] = v.astype(BF16)
        va_scr[:, ATTN_VD:2 * ATTN_VD] = ones
        if lctx:
            kcT_scrs[hh][...] = ckT_ref[0, hh].astype(BF16)
            stage_ctx_values(va_scr, hh)

        def softmax_pv(qb, kk_scr=kk_scr, va_scr=va_scr, hh=hh):
            s = _dot_nt(qb, kk_scr[...])
            if lctx:
                s = jnp.concatenate([s, _dot(qb, kcT_scrs[hh][...])], axis=1)
            p = jnp.exp2(s - jnp.max(s, axis=-1, keepdims=True)).astype(BF16)
            a = _dot(p, va_scr[...])
            return a[:, 0:ATTN_VD] / a[:, ATTN_VD:2 * ATTN_VD]

        o = softmax_pv(q1_scr[...]) - lam * softmax_pv(q2_scr[...])
        o = o * lax.rsqrt(jnp.mean(o * o, axis=-1, keepdims=True) + EPS) * sub_gain
        o_ref[:, cols] = o.astype(o_ref.dtype)


def _attn(p2d, lamp, qg, kg, sg, batch, lq, lam_init, hps, rope_tabs=None, ctx=None, emit_kv=False):
    m = p2d.shape[0]
    lctx = 0 if ctx is None else ctx[0].shape[-1]
    width = hps * HEAD_W
    qb, kb, vb = P_Q // width, P_K // width, P_V // width
    const = lambda b, h: (0, 0)
    in_specs = [pl.BlockSpec((4, ATTN_DH), const),
                pl.BlockSpec((1, HEAD_W), const),
                pl.BlockSpec((1, HEAD_W), const),
                pl.BlockSpec((1, ATTN_VD), const),
                pl.BlockSpec((lq, width), lambda b, h: (b, qb + h)),
                pl.BlockSpec((lq, width), lambda b, h: (b, kb + h)),
                pl.BlockSpec((lq, width), lambda b, h: (b, vb + h))]
    args = [lamp, qg, kg, sg, p2d, p2d, p2d]
    if rope_tabs is not None:
        in_specs += [pl.BlockSpec((lq, HEAD_W), const)] * 3
        args += list(rope_tabs)
    if ctx is not None:
        in_specs += [pl.BlockSpec((1, hps, HEAD_W, lctx), lambda b, h: (b, h, 0, 0)),
                     pl.BlockSpec((1, lctx * ATTN_HEADS, ATTN_VD), lambda b, h: (b, 0, 0))]
        args += list(ctx)
    head_out = pl.BlockSpec((lq, width), lambda b, h: (b, h))
    out_specs = [head_out]
    out_shape = [jax.ShapeDtypeStruct((m, D_MODEL), ACT_DTYPE)]
    if emit_kv:
        out_specs += [pl.BlockSpec((1, hps, HEAD_W, lq), lambda b, h: (b, h, 0, 0)),
                      pl.BlockSpec((1, lq * ATTN_HEADS, ATTN_VD), lambda b, h: (b, 0, 0))]
        out_shape += [jax.ShapeDtypeStruct((batch, ATTN_HEADS, HEAD_W, lq), F32),
                      jax.ShapeDtypeStruct((batch, lq * ATTN_HEADS, ATTN_VD), F32)]
    scratch = hps * [pltpu.VMEM((lq, HEAD_W), BF16),
                     pltpu.VMEM((lq, HEAD_W), BF16),
                     pltpu.VMEM((lq, HEAD_W), BF16),
                     pltpu.VMEM((lq + lctx, 2 * ATTN_VD), BF16)]
    if lctx:
        scratch += hps * [pltpu.VMEM((HEAD_W, lctx), BF16)]
    kern = functools.partial(_attn_kernel, lq=lq, lctx=lctx, hps=hps, lam_init=lam_init,
                             rope=rope_tabs is not None, emit_kv=emit_kv)
    return pl.pallas_call(
        kern,
        grid=(batch, ATTN_HEADS // hps),
        in_specs=in_specs,
        out_specs=out_specs,
        out_shape=out_shape,
        scratch_shapes=scratch,
        compiler_params=_cparams(("arbitrary", "arbitrary")),
        name="attn",
    )(*args)


def _ssd_kernel(*refs, seq, has_h0, gps):
    it = iter(refs)
    x_ref, b_ref, c_ref = next(it), next(it), next(it)
    cwx_ref, cwb_ref, cwc_ref = next(it), next(it), next(it)
    cbx_ref, cbb_ref, cbc_ref = next(it), next(it), next(it)
    dexp_ref, cs_ref, rtT_ref = next(it), next(it), next(it)
    if has_h0:
        h0f_ref, h0b_ref = next(it), next(it)
    y_ref, hf_ref, hb_ref = next(it), next(it), next(it)
    group_scr = [tuple(next(it) for _ in range(10)) for _ in range(gps)]

    g0 = pl.program_id(1) * gps
    nc = seq // CHUNK
    x_lanes = [slice(gi * GROUP_W, (gi + 1) * GROUP_W) for gi in range(gps)]
    n_lanes = [slice(gi * SSM_STATE, (gi + 1) * SSM_STATE) for gi in range(gps)]
    blk = lax.broadcasted_iota(jnp.int32, (CHUNK, GROUP_W), 1) // SSM_HEADDIM

    win_rows = CHUNK + 2 * CONV_EDGE
    sel_r = lax.broadcasted_iota(jnp.int32, (CHUNK, win_rows), 0)
    sel_c = lax.broadcasted_iota(jnp.int32, (CHUNK, win_rows), 1)
    taps = [d for d in range(-(D_CONV // 2), D_CONV // 2 + 1) if d != 0]
    shift_sel = {d: jnp.where(sel_c == sel_r + CONV_EDGE + d, 1.0, 0.0).astype(BF16) for d in taps}

    def conv_chunk(c, sources, shift_on_mxu):
        r0 = pl.multiple_of(c * CHUNK, CHUNK)
        p0 = pl.multiple_of(jnp.maximum(r0 - CONV_EDGE, 0), CONV_EDGE)
        n0 = pl.multiple_of(jnp.minimum(r0 + CHUNK, seq - CONV_EDGE), CONV_EDGE)

        def gather(rows_of):
            parts = [rows_of(*src) for src in sources]
            return parts[0] if len(parts) == 1 else jnp.concatenate(parts, axis=1)

        assert all(src[0].dtype == BF16 for src in sources)
        prev = gather(lambda ref, w_ref, b_ref, lanes: ref[pl.ds(p0, CONV_EDGE), lanes])
        nxt = gather(lambda ref, w_ref, b_ref, lanes: ref[pl.ds(n0, CONV_EDGE), lanes])
        main = gather(lambda ref, w_ref, b_ref, lanes: ref[pl.ds(r0, CHUNK), lanes])
        w = [gather(lambda ref, w_ref, b_ref, lanes, j=j: w_ref[j:j + 1, lanes]) for j in range(D_CONV)]
        bias = gather(lambda ref, w_ref, b_ref, lanes: b_ref[:, lanes])
        acc = bias + main.astype(F32) * w[D_CONV // 2]
        if shift_on_mxu:
            zero = jnp.zeros_like(prev)
            win = jnp.concatenate([jnp.where(c > 0, prev, zero), main,
                                   jnp.where(c < nc - 1, nxt, zero)], axis=0)
            for d in taps:
                acc = acc + _dot(shift_sel[d], win) * w[d + D_CONV // 2]
        else:
            rows = CHUNK + 2 * CONV_PAD
            win = jnp.concatenate(
                [jnp.where(c > 0, prev.astype(F32)[CONV_EDGE - CONV_PAD:, :], 0.0), main.astype(F32),
                 jnp.where(c < nc - 1, nxt.astype(F32)[:CONV_PAD, :], 0.0)], axis=0)
            for d in taps:
                tap = pltpu.roll(win, (-d) % rows, axis=0)[CONV_PAD:CONV_PAD + CHUNK, :]
                acc = acc + tap * w[d + D_CONV // 2]
        return _silu(acc)

    head_masks = [jnp.where(blk == j, 1.0, 0.0).astype(BF16) for j in range(HEADS_PER_GROUP)]

    def conv_body(c, carry):
        sl = pl.ds(pl.multiple_of(c * CHUNK, CHUNK), CHUNK)
        for gi in range(gps):
            xm_s, bT_s, cc_s, _, _, _, yacc, g_s, _, _ = group_scr[gi]
            xv = conv_chunk(c, [(x_ref, cwx_ref, cbx_ref, x_lanes[gi])], nc > 2)
            yacc[sl, :] = dexp_ref[:, x_lanes[gi]] * xv
            xb = xv.astype(BF16)
            for j in range(HEADS_PER_GROUP):
                xm_s[c, j * CHUNK:(j + 1) * CHUNK, :] = xb * head_masks[j]
            bc = conv_chunk(c, [(b_ref, cwb_ref, cbb_ref, n_lanes[gi]),
                                (c_ref, cwc_ref, cbc_ref, n_lanes[gi])], False)
            bT = bc[:, 0:SSM_STATE].T.astype(BF16)
            cc = bc[:, SSM_STATE:2 * SSM_STATE].astype(BF16)
            bT_s[c] = bT
            cc_s[sl, :] = cc
            g_s[c] = _dot(cc, bT)
        return carry

    lax.fori_loop(0, nc, conv_body, 0, unroll=2)

    for gi in range(gps):
        csg_s, stf, stb = group_scr[gi][3:6]
        shift = (LANES - HEADS_PER_GROUP * (g0 + gi)) % LANES
        csg_s[...] = pltpu.roll(cs_ref[...], shift, axis=1)
        if has_h0:
            stf[...] = h0f_ref[0, x_lanes[gi], :].T
            stb[...] = h0b_ref[0, x_lanes[gi], :].T
        else:
            stf[...] = jnp.zeros_like(stf)
            stb[...] = jnp.zeros_like(stb)

    row = lax.broadcasted_iota(jnp.int32, (CHUNK, CHUNK), 0)
    col = lax.broadcasted_iota(jnp.int32, (CHUNK, CHUNK), 1)
    lane_lo = lax.broadcasted_iota(jnp.int32, (CHUNK, LANES), 1) < SSM_HEADDIM

    def decay_terms(c, gi):
        xm_s, bT_s, cc_s, csg_s, _, _, yacc, g_s, upd_s, e_s = group_scr[gi]
        g = g0 + gi
        sl = pl.ds(pl.multiple_of(c * CHUNK, CHUNK), CHUNK)
        bT = bT_s[c]
        cs = csg_s[sl, :]
        x_heads = xm_s[c]
        dec_sum = [None] * HEADS_PER_GROUP
        for d, (lane0, mask, edge) in enumerate(((0, col <= row, CHUNK - 1), (SSM_HEADS, col >= row, 0))):
            upd_lhs, cols = [], []
            for j in range(HEADS_PER_GROUP):
                cb = jnp.broadcast_to(cs[:, lane0 + j:lane0 + j + 1], (CHUNK, CHUNK))
                rrow = rtT_ref[c, pl.ds(lane0 + HEADS_PER_GROUP * g + j, 1), :]
                dec = jnp.exp2(jnp.where(mask, cb - rrow, -jnp.inf))
                dec_sum[j] = dec if dec_sum[j] is None else dec_sum[j] + dec
                w_edge = dec[edge:edge + 1, :]
                upd_lhs.append(bT * w_edge.astype(BF16))
                cols.append(cb)
            upd_s[d, c] = _dot(jnp.concatenate(upd_lhs, axis=1), x_heads)
            cs_e = jnp.concatenate([jnp.where(lane_lo, cols[2 * t], cols[2 * t + 1])
                                    for t in range(HEADS_PER_GROUP // 2)], axis=1)
            e_s[d, c] = jnp.exp2(cs_e)
        gmat = g_s[c]
        diag_lhs = [(gmat * dec_sum[j]).astype(BF16) for j in range(HEADS_PER_GROUP)]
        yacc[sl, :] = yacc[sl, :] + _dot(jnp.concatenate(diag_lhs, axis=1), x_heads)

    def decay_body(c, carry):
        for gi in range(gps):
            decay_terms(c, gi)
        return carry

    lax.fori_loop(0, nc, decay_body, 0, unroll=2)

    def state_step(c, d, gi):
        _, _, cc_s, _, stf, stb, yacc, _, upd_s, e_s = group_scr[gi]
        st_ref = stb if d else stf
        edge = 0 if d else CHUNK - 1
        sl = pl.ds(pl.multiple_of(c * CHUNK, CHUNK), CHUNK)
        e = e_s[d, c]
        st = st_ref[...]
        yacc[sl, :] = yacc[sl, :] + e * _dot(cc_s[sl, :], st.astype(BF16))
        st_ref[...] = e[edge:edge + 1, :] * st + upd_s[d, c]

    def state_body(i, carry):
        for gi in range(gps):
            state_step(i, 0, gi)
            state_step(nc - 1 - i, 1, gi)
        return carry

    lax.fori_loop(0, nc, state_body, 0, unroll=2)

    for gi in range(gps):
        stf, stb, yacc = group_scr[gi][4:7]
        y_ref[:, x_lanes[gi]] = yacc[...].astype(y_ref.dtype)
        hf_ref[0, x_lanes[gi], :] = stf[...].T
        hb_ref[0, x_lanes[gi], :] = stb[...].T


def _ssd(p2d, cs, rtT, conv_w, conv_b, dexp, batch, seq, gps, h0=None):
    m = p2d.shape[0]
    nc = seq // CHUNK
    xw, nw = gps * GROUP_W, gps * SSM_STATE
    xb, bb, cb = P_X // xw, P_B // nw, P_C // nw
    cw_b0 = D_INNER // nw
    cw_c0 = cw_b0 + SSM_GROUPS // gps
    in_specs = [pl.BlockSpec((seq, xw), lambda b, g: (b, xb + g)),
                pl.BlockSpec((seq, nw), lambda b, g: (b, bb + g)),
                pl.BlockSpec((seq, nw), lambda b, g: (b, cb + g)),
                pl.BlockSpec((D_CONV, xw), lambda b, g: (0, g)),
                pl.BlockSpec((D_CONV, nw), lambda b, g: (0, cw_b0 + g)),
                pl.BlockSpec((D_CONV, nw), lambda b, g: (0, cw_c0 + g)),
                pl.BlockSpec((1, xw), lambda b, g: (0, g)),
                pl.BlockSpec((1, nw), lambda b, g: (0, cw_b0 + g)),
                pl.BlockSpec((1, nw), lambda b, g: (0, cw_c0 + g)),
                pl.BlockSpec((1, xw), lambda b, g: (0, g)),
                pl.BlockSpec((seq, LANES), lambda b, g: (b, 0)),
                pl.BlockSpec((nc, CHUNK, CHUNK), lambda b, g: (b, 0, 0))]
    args = [p2d, p2d, p2d, conv_w, conv_w, conv_w, conv_b, conv_b, conv_b, dexp, cs, rtT]
    state_spec = pl.BlockSpec((1, xw, SSM_STATE), lambda b, g: (b, g, 0))
    if h0 is not None:
        in_specs += [state_spec, state_spec]
        args += list(h0)
    state_shape = jax.ShapeDtypeStruct((batch, D_INNER, SSM_STATE), F32)
    kern = functools.partial(_ssd_kernel, seq=seq, has_h0=h0 is not None, gps=gps)
    return pl.pallas_call(
        kern,
        grid=(batch, SSM_GROUPS // gps),
        in_specs=in_specs,
        out_specs=[pl.BlockSpec((seq, xw), lambda b, g: (b, g)), state_spec, state_spec],
        out_shape=[jax.ShapeDtypeStruct((m, D_INNER), ACT_DTYPE), state_shape, state_shape],
        scratch_shapes=gps * [pltpu.VMEM((nc, HEADS_PER_GROUP * CHUNK, GROUP_W), BF16),
                              pltpu.VMEM((nc, SSM_STATE, CHUNK), BF16),
                              pltpu.VMEM((seq, SSM_STATE), BF16),
                              pltpu.VMEM((seq, LANES), F32),
                              pltpu.VMEM((SSM_STATE, GROUP_W), F32),
                              pltpu.VMEM((SSM_STATE, GROUP_W), F32),
                              pltpu.VMEM((seq, GROUP_W), F32),
                              pltpu.VMEM((nc, CHUNK, CHUNK), F32),
                              pltpu.VMEM((2, nc, SSM_STATE, GROUP_W), F32),
                              pltpu.VMEM((2, nc, CHUNK, GROUP_W), F32)],
        compiler_params=_cparams(("arbitrary", "arbitrary")),
        name="ssd",
    )(*args)


def _merge_kernel(x_ref, mod_ref, o_ref, y_ref, z_ref, ga_ref, gb_ref, sg_ref,
                  wa_ref, wb_ref, wo_ref, out_ref):
    out_a = _dot(o_ref[...], wa_ref[...])
    yz = y_ref[...].astype(F32) * _silu(z_ref[...].astype(F32))
    yn = yz * lax.rsqrt(jnp.mean(yz * yz, axis=-1, keepdims=True) + EPS) * sg_ref[...]
    out_b = _dot(yn.astype(BF16), wb_ref[...])
    merged = (jax.nn.sigmoid(ga_ref[...].astype(F32)) * out_a
              + jax.nn.sigmoid(gb_ref[...].astype(F32)) * out_b)
    gate1 = mod_ref[0, :, 2 * D_MODEL:3 * D_MODEL]
    out_ref[...] = x_ref[...] + gate1 * _dot(merged.astype(BF16), wo_ref[...])


def _merge(x2d, mod, o, y, p2d, ssm_g, wa, wb, wo, tm):
    m = x2d.shape[0]
    rows_per_mod = m // mod.shape[0]
    return pl.pallas_call(
        _merge_kernel,
        grid=(m // tm,),
        in_specs=[pl.BlockSpec((tm, D_MODEL), lambda i: (i, 0)),
                  pl.BlockSpec((1, 1, 6 * D_MODEL), lambda i: ((i * tm) // rows_per_mod, 0, 0)),
                  pl.BlockSpec((tm, D_MODEL), lambda i: (i, 0)),
                  pl.BlockSpec((tm, D_INNER), lambda i: (i, 0)),
                  pl.BlockSpec((tm, D_INNER), lambda i: (i, P_Z // D_INNER)),
                  pl.BlockSpec((tm, D_MODEL), lambda i: (i, P_GA // D_MODEL)),
                  pl.BlockSpec((tm, D_MODEL), lambda i: (i, P_GB // D_MODEL)),
                  _resident((1, D_INNER)),
                  _resident((D_MODEL, D_MODEL)),
                  _resident((D_INNER, D_MODEL)),
                  _resident((D_MODEL, D_MODEL))],
        out_specs=pl.BlockSpec((tm, D_MODEL), lambda i: (i, 0)),
        out_shape=jax.ShapeDtypeStruct((m, D_MODEL), F32),
        compiler_params=_cparams(("arbitrary",)),
        name="merge",
    )(x2d, mod, o, y, p2d, p2d, p2d, ssm_g, wa, wb, wo)


def _ffn_kernel(x_ref, mod_ref, g_ref, wg_ref, wu_ref, wd_ref, out_ref):
    x = x_ref[...]
    y = x * lax.rsqrt(jnp.mean(x * x, axis=-1, keepdims=True) + EPS) * g_ref[...]
    shift = mod_ref[0, :, 3 * D_MODEL:4 * D_MODEL]
    scale = mod_ref[0, :, 4 * D_MODEL:5 * D_MODEL]
    gate = mod_ref[0, :, 5 * D_MODEL:6 * D_MODEL]
    h = (y * (1.0 + scale) + shift).astype(BF16)
    f = (_silu(_dot(h, wg_ref[...])) * _dot(h, wu_ref[...])).astype(BF16)
    out_ref[...] = x + gate * _dot(f, wd_ref[...])


def _ffn(x2d, mod, g2, wg, wu, wd, tm):
    m = x2d.shape[0]
    rows_per_mod = m // mod.shape[0]
    return pl.pallas_call(
        _ffn_kernel,
        grid=(m // tm,),
        in_specs=[pl.BlockSpec((tm, D_MODEL), lambda i: (i, 0)),
                  pl.BlockSpec((1, 1, 6 * D_MODEL), lambda i: ((i * tm) // rows_per_mod, 0, 0)),
                  _resident((1, D_MODEL)),
                  _resident((D_MODEL, D_FF)),
                  _resident((D_MODEL, D_FF)),
                  _resident((D_FF, D_MODEL))],
        out_specs=pl.BlockSpec((tm, D_MODEL), lambda i: (i, 0)),
        out_shape=jax.ShapeDtypeStruct((m, D_MODEL), F32),
        compiler_params=_cparams(("arbitrary",)),
        name="ffn",
    )(x2d, mod, g2, wg, wu, wd)


NORM_ROWS = 1024
PROJ_ROWS = 4096
MIX_ROWS = 512
ATTN_STEP_ROWS = 2048
ATTN_STEP_ROWS_CTX = 4096


def _tiles(rows, n_mod, seq, has_ctx):
    rows_per_mod = rows // n_mod
    attn_rows = ATTN_STEP_ROWS_CTX if has_ctx else ATTN_STEP_ROWS
    return dict(
        norm_rows=min(NORM_ROWS, rows_per_mod),
        proj_rows=min(PROJ_ROWS, rows),
        mix_rows=min(MIX_ROWS, rows_per_mod),
        heads_per_step=max(1, min(ATTN_HEADS, attn_rows // seq)),
        groups_per_step=SSD_GROUPS_PER_STEP,
    )


def _rope_tables(n_tok):
    pos = np.arange(n_tok)
    row = (pos // GRID_W).astype(np.float32)
    colp = (pos % GRID_W).astype(np.float32)
    inv = (ROPE_BASE ** (-np.arange(0, ROPE_AXIS_DIM, 2, dtype=np.float32) / ROPE_AXIS_DIM)).astype(np.float32)
    ang_r = row[:, None] * inv[None, :]
    ang_c = colp[:, None] * inv[None, :]
    zero = np.zeros_like(ang_r)
    cos = np.concatenate([np.cos(ang_r)] * 2 + [np.cos(ang_c)] * 2, axis=-1)
    sup = np.concatenate([-np.sin(ang_r), zero, -np.sin(ang_c), zero], axis=-1)
    sdn = np.concatenate([zero, np.sin(ang_r), zero, np.sin(ang_c)], axis=-1)
    return tuple(np.tile(t, (1, 2)).astype(np.float32) for t in (cos, sup, sdn))


def _pad_lanes(v):
    flat = v.reshape(1, -1).astype(F32)
    return jnp.pad(flat, ((0, 0), (0, LANES - flat.shape[1])))


def kernel(x_prompt, x_sample, c, cache_k, cache_v, state_ssm_fwd, state_ssm_bwd, c_ctx, norm1_g, norm2_g, w_ada, b_ada, w_in, q_norm_g, k_norm_g, lambda_q1, lambda_k1, lambda_q2, lambda_k2, attn_sub_g, conv_w, conv_b, A_log, dt_bias, D_skip, ssm_norm_g, w_branch_a, w_branch_b, w_out, w_ffn_gate, w_ffn_up, w_ffn_down):
    depth = norm1_g.shape[0]
    assert depth == 1, "single-layer kernel"
    l = 0
    lam_init = 0.8 - 0.6 * math.exp(-0.3 * l)
    nb, seq, _ = x_prompt.shape
    db, dseq, _ = x_sample.shape

    cond = jnp.concatenate([c_ctx[None, :], c, jnp.zeros((SUBLANES - 1 - db, D_MODEL), F32)], axis=0)
    mod = _ada(cond, w_ada[l], b_ada[l][None, :])
    mod_ctx = mod[0:1].reshape(1, 1, 6 * D_MODEL)
    mod_lat = mod[1:1 + db].reshape(db, 1, 6 * D_MODEL)

    w = jnp.transpose(w_in[l])
    w_gates = w[W_G0:W_G0 + 2 * D_MODEL]
    w_dt = jnp.pad(w[W_DT0:W_G0], ((0, LANES - 2 * SSM_HEADS), (0, 0)))
    wa = w_branch_a[l].astype(BF16)
    wb = w_branch_b[l].astype(BF16)
    wo = w_out[l].astype(BF16)
    wg = w_ffn_gate[l].astype(BF16)
    wu = w_ffn_up[l].astype(BF16)
    wd = w_ffn_down[l].astype(BF16)
    g1 = norm1_g[l][None, :]
    g2 = norm2_g[l][None, :]
    qg = jnp.tile(q_norm_g[l], 2)[None, :]
    kg = jnp.tile(k_norm_g[l], 2)[None, :]
    sg = attn_sub_g[l][None, :]
    lamp = jnp.stack([lambda_q1[l], lambda_k1[l], lambda_q2[l], lambda_k2[l]], axis=0)
    bias_row = _pad_lanes(dt_bias[l])
    alog_row = _pad_lanes(A_log[l])
    dexp = jnp.repeat(D_skip[l], SSM_HEADDIM)[None, :]
    cw = conv_w[l]
    cb = conv_b[l][None, :]
    ssm_g = ssm_norm_g[l][None, :]

    def layer(x, mod_rows, rope_tabs, ctx, h0, emit_kv):
        batch, sl, _ = x.shape
        x2d = x.reshape(batch * sl, D_MODEL)
        tiles = _tiles(batch * sl, mod_rows.shape[0], sl, ctx is not None)
        h, dt_raw = _norm1(x2d, mod_rows, g1, w_dt, tm=tiles["norm_rows"])
        p2d = _proj(h, w, w_gates, tm=tiles["proj_rows"])
        cs, rtT = _dtprep(dt_raw, bias_row, alog_row)
        attn_out = _attn(p2d, lamp, qg, kg, sg, batch, sl, lam_init, tiles["heads_per_step"],
                         rope_tabs, ctx, emit_kv)
        y, hf, hb = _ssd(p2d, cs, rtT, cw, cb, dexp, batch, sl, tiles["groups_per_step"], h0)
        x1 = _merge(x2d, mod_rows, attn_out[0], y, p2d, ssm_g, wa, wb, wo, tm=tiles["mix_rows"])
        x2 = _ffn(x1, mod_rows, g2, wg, wu, wd, tm=tiles["mix_rows"])
        return x2.reshape(batch, sl, D_MODEL), attn_out[1:], hf, hb

    y_prompt, kv, hf, hb = layer(x_prompt, mod_ctx, None, None, None, True)
    ctx_kT = jnp.transpose(cache_k[:, l], (0, 2, 3, 4, 1)).reshape(db, ATTN_HEADS, HEAD_W, -1)
    ctx_v = cache_v[:, l].reshape(db, -1, ATTN_VD)
    h0 = (state_ssm_fwd[:, l].reshape(db, D_INNER, SSM_STATE),
          state_ssm_bwd[:, l].reshape(db, D_INNER, SSM_STATE))
    y_sample, _, _, _ = layer(x_sample, mod_lat, _rope_tables(dseq), (ctx_kT, ctx_v), h0, False)

    knT = kv[0].reshape(nb, ATTN_HEADS, 2, ATTN_DH, seq)
    new_cache_k = jnp.transpose(knT, (0, 4, 1, 2, 3)).reshape(nb, 1, seq, ATTN_HEADS, 2, ATTN_DH)
    new_cache_v = kv[1].reshape(nb, 1, seq, ATTN_HEADS, ATTN_VD)
    new_hf = hf.reshape(nb, 1, SSM_HEADS, SSM_HEADDIM, SSM_STATE)
    new_hb = hb.reshape(nb, 1, SSM_HEADS, SSM_HEADDIM, SSM_STATE)
    return (y_prompt, y_sample, new_cache_k, new_cache_v, new_hf, new_hb)
```

```python
import functools
import math

import numpy as np
import jax
import jax.numpy as jnp
from jax import lax
from jax.experimental import pallas as pl
from jax.experimental.pallas import tpu as pltpu

F32 = jnp.float32
BF16 = jnp.bfloat16

D_MODEL = 1024
GRID_W = 64
ATTN_HEADS = 8
ATTN_DH = 64
ATTN_VD = 128
HEAD_W = 2 * ATTN_DH
ROPE_BASE = 10000.0
ROPE_AXIS_DIM = ATTN_DH // 2
ROPE_HALF = ROPE_AXIS_DIM // 2
D_INNER = 2 * D_MODEL
SSM_HEADDIM = 64
SSM_HEADS = D_INNER // SSM_HEADDIM
SSM_GROUPS = 8
HEADS_PER_GROUP = SSM_HEADS // SSM_GROUPS
GROUP_W = HEADS_PER_GROUP * SSM_HEADDIM
SSM_STATE = 128
D_CONV = 5
CONV_PAD = 8
CONV_EDGE = 16
CHUNK = 128
D_FF = ((8 * D_MODEL // 3 + 255) // 256) * 256
EPS = 1e-6
LANES = 128
SUBLANES = 8
LOG2E = math.log2(math.e)

P_Z = 0
P_GA = P_Z + D_INNER
P_GB = P_GA + D_MODEL
P_X = P_GB + D_MODEL
P_B = P_X + D_INNER
P_C = P_B + SSM_GROUPS * SSM_STATE
P_Q = P_C + SSM_GROUPS * SSM_STATE
P_K = P_Q + D_MODEL
P_V = P_K + D_MODEL
NP = P_V + D_MODEL
PROJ_TN = 512
ACT_DTYPE = BF16

W_Q0 = 0
W_Z0 = 3 * D_MODEL
W_XBC0 = W_Z0 + D_INNER
W_DT0 = W_XBC0 + D_INNER + 2 * SSM_GROUPS * SSM_STATE
W_G0 = W_DT0 + 2 * SSM_HEADS

VMEM_LIMIT = 56 * 1024 * 1024


def _cparams(sem):
    return pltpu.CompilerParams(dimension_semantics=sem, vmem_limit_bytes=VMEM_LIMIT)


def _silu(x):
    return x * jax.nn.sigmoid(x)


def _dot(a, b):
    return jnp.dot(a, b, preferred_element_type=F32)


def _dot_nt(a, b):
    return lax.dot_general(a, b, (((1,), (1,)), ((), ())), preferred_element_type=F32)


def _split3(x):
    hi = x.astype(BF16)
    r = x - hi.astype(F32)
    mid = r.astype(BF16)
    lo = (r - mid.astype(F32)).astype(BF16)
    return hi, mid, lo


def _resident(shape):
    return pl.BlockSpec(shape, lambda *_: (0,) * len(shape), pipeline_mode=pl.Buffered(1))


def _ada_kernel(cond_ref, w_ref, b_ref, o_ref):
    s = _silu(cond_ref[...]).astype(BF16)
    o_ref[...] = _dot(s, w_ref[...].astype(BF16)) + b_ref[...]


def _ada(cond, w_ada, b_ada):
    rows = cond.shape[0]
    n = w_ada.shape[1]
    tn = 1024
    return pl.pallas_call(
        _ada_kernel,
        grid=(n // tn,),
        in_specs=[pl.BlockSpec((rows, D_MODEL), lambda j: (0, 0)),
                  pl.BlockSpec((D_MODEL, tn), lambda j: (0, j)),
                  pl.BlockSpec((1, tn), lambda j: (0, j))],
        out_specs=pl.BlockSpec((rows, tn), lambda j: (0, j)),
        out_shape=jax.ShapeDtypeStruct((rows, n), F32),
        compiler_params=_cparams(("arbitrary",)),
        name="ada",
    )(cond, w_ada, b_ada)


def _norm1_kernel(x_ref, mod_ref, g_ref, wdt_ref, h_ref, dt_ref):
    x = x_ref[...]
    y = x * lax.rsqrt(jnp.mean(x * x, axis=-1, keepdims=True) + EPS) * g_ref[...]
    shift = mod_ref[0, :, 0:D_MODEL]
    scale = mod_ref[0, :, D_MODEL:2 * D_MODEL]
    h = (y * (1.0 + scale) + shift).astype(BF16)
    h_ref[...] = h
    dt_ref[...] = _dot_nt(h, wdt_ref[...].astype(BF16))


def _norm1(x2d, mod, g1, w_dt, tm):
    m = x2d.shape[0]
    rows_per_mod = m // mod.shape[0]
    return pl.pallas_call(
        _norm1_kernel,
        grid=(m // tm,),
        in_specs=[pl.BlockSpec((tm, D_MODEL), lambda i: (i, 0)),
                  pl.BlockSpec((1, 1, 6 * D_MODEL), lambda i: ((i * tm) // rows_per_mod, 0, 0)),
                  _resident((1, D_MODEL)),
                  _resident((LANES, D_MODEL))],
        out_specs=[pl.BlockSpec((tm, D_MODEL), lambda i: (i, 0)),
                   pl.BlockSpec((tm, LANES), lambda i: (i, 0))],
        out_shape=[jax.ShapeDtypeStruct((m, D_MODEL), BF16),
                   jax.ShapeDtypeStruct((m, LANES), F32)],
        compiler_params=_cparams(("arbitrary",)),
        name="norm1",
    )(x2d, mod, g1, w_dt)


_GATE_TILE0 = P_GA // PROJ_TN
_N_GATE_TILES = 2 * D_MODEL // PROJ_TN
_GATE_ROW_OFFSET = W_G0 % PROJ_TN


def _w_in_tile(j):
    z_tiles = D_INNER // PROJ_TN
    xbc_tile0 = (P_X // PROJ_TN)
    qkv_tile0 = (P_Q // PROJ_TN)
    z_src = jnp.minimum(j, z_tiles - 1) + W_Z0 // PROJ_TN
    xbc_src = j - xbc_tile0 + W_XBC0 // PROJ_TN
    qkv_src = j - qkv_tile0 + W_Q0 // PROJ_TN
    return jnp.where(j < xbc_tile0, z_src, jnp.where(j < qkv_tile0, xbc_src, qkv_src))


def _proj_kernel(h_ref, win_ref, wg_lo_ref, wg_hi_ref, p_ref, w_scr, *, tm):
    j = pl.program_id(0)
    i = pl.program_id(1)
    is_gate = jnp.logical_and(j >= _GATE_TILE0, j < _GATE_TILE0 + _N_GATE_TILES)

    @pl.when(jnp.logical_and(i == 0, is_gate))
    def _():
        w_scr[0:PROJ_TN - _GATE_ROW_OFFSET, :] = wg_lo_ref[_GATE_ROW_OFFSET:, :].astype(BF16)
        w_scr[PROJ_TN - _GATE_ROW_OFFSET:, :] = wg_hi_ref[0:_GATE_ROW_OFFSET, :].astype(BF16)

    @pl.when(jnp.logical_and(i == 0, jnp.logical_not(is_gate)))
    def _():
        w_scr[...] = win_ref[...].astype(BF16)

    r0 = pl.multiple_of(i * tm, tm)
    p_ref[...] = _dot_nt(h_ref[pl.ds(r0, tm), :], w_scr[...]).astype(p_ref.dtype)


def _proj(h, w_in_t, tm):
    m = h.shape[0]
    tn = PROJ_TN

    def gate_block(j):
        return W_G0 // tn + jnp.clip(j - _GATE_TILE0, 0, _N_GATE_TILES - 1)

    return pl.pallas_call(
        functools.partial(_proj_kernel, tm=tm),
        grid=(NP // tn, m // tm),
        in_specs=[_resident((m, D_MODEL)),
                  pl.BlockSpec((tn, D_MODEL), lambda j, i: (_w_in_tile(j), 0)),
                  pl.BlockSpec((tn, D_MODEL), lambda j, i: (gate_block(j), 0)),
                  pl.BlockSpec((tn, D_MODEL), lambda j, i: (gate_block(j) + 1, 0))],
        out_specs=pl.BlockSpec((tm, tn), lambda j, i: (i, j)),
        out_shape=jax.ShapeDtypeStruct((m, NP), ACT_DTYPE),
        scratch_shapes=[pltpu.VMEM((tn, D_MODEL), BF16)],
        compiler_params=_cparams(("arbitrary", "arbitrary")),
        name="proj",
    )(h, w_in_t, w_in_t, w_in_t)


DTPREP_CHUNKS = 8
SSD_GROUPS_PER_STEP = 4

def _dtprep_kernel(dtr_ref, bias_ref, alog_ref, cs_ref, rtT_ref):
    row = lax.broadcasted_iota(jnp.int32, (CHUNK, CHUNK), 0)
    col = lax.broadcasted_iota(jnp.int32, (CHUNK, CHUNK), 1)
    tril = jnp.where(col <= row, 1.0, 0.0).astype(BF16)
    triu = jnp.where(col >= row, 1.0, 0.0).astype(BF16)
    neg_a = -jnp.exp(alog_ref[...])
    for c in range(DTPREP_CHUNKS):
        sl = slice(c * CHUNK, (c + 1) * CHUNK)
        x = dtr_ref[sl, :] + bias_ref[...]
        dt = jnp.maximum(x, 0.0) + jnp.log1p(jnp.exp(-jnp.abs(x)))
        hi, mid, lo = _split3(dt * neg_a)
        cf = _dot(tril, hi) + _dot(tril, mid) + _dot(tril, lo)
        cb = _dot(triu, hi) + _dot(triu, mid) + _dot(triu, lo)
        cs = jnp.where(col < SSM_HEADS, cf, cb)
        cs_ref[sl, :] = cs * LOG2E
        rtT_ref[c] = ((cs - jnp.log(dt)) * LOG2E).T


def _dtprep(dt_raw, bias_row, alog_row):
    m = dt_raw.shape[0]
    rows = DTPREP_CHUNKS * CHUNK
    return pl.pallas_call(
        _dtprep_kernel,
        grid=(m // rows,),
        in_specs=[pl.BlockSpec((rows, LANES), lambda i: (i, 0)),
                  _resident((1, LANES)),
                  _resident((1, LANES))],
        out_specs=[pl.BlockSpec((rows, LANES), lambda i: (i, 0)),
                   pl.BlockSpec((DTPREP_CHUNKS, CHUNK, CHUNK), lambda i: (i, 0, 0))],
        out_shape=[jax.ShapeDtypeStruct((m, LANES), F32),
                   jax.ShapeDtypeStruct((m // CHUNK, CHUNK, CHUNK), F32)],
        compiler_params=_cparams(("arbitrary",)),
        name="dtprep",
    )(dt_raw, bias_row, alog_row)


def _half_norm(x, g, half_ones):
    ss = x * x
    hi = ss.astype(BF16)
    lo = (ss - hi.astype(F32)).astype(BF16)
    ms = _dot(jnp.concatenate([hi, lo], axis=1), half_ones) * (1.0 / ATTN_DH)
    return x * lax.rsqrt(ms + EPS) * g


def _rope(x, cos, sin_up, sin_dn):
    x_up = pltpu.roll(x, HEAD_W - ROPE_HALF, axis=1)
    x_dn = pltpu.roll(x, ROPE_HALF, axis=1)
    return x * cos + x_up * sin_up + x_dn * sin_dn


def _attn_kernel(*refs, lq, lctx, hps, lam_init, rope, emit_kv):
    it = iter(refs)
    lamp_ref, qg_ref, kg_ref, sg_ref, q_ref, k_ref, v_ref = (next(it) for _ in range(7))
    if rope:
        cos_ref, sup_ref, sdn_ref = next(it), next(it), next(it)
    if lctx:
        ckT_ref, cv_ref = next(it), next(it)
    o_ref = next(it)
    if emit_kv:
        knT_ref, vo_ref = next(it), next(it)
    head_scr = [tuple(next(it) for _ in range(4)) for _ in range(hps)]
    if lctx:
        kcT_scrs = [next(it) for _ in range(hps)]

    lp = lamp_ref[...]
    lam = (jnp.exp(jnp.sum(lp[0:1] * lp[1:2], axis=-1, keepdims=True))
           - jnp.exp(jnp.sum(lp[2:3] * lp[3:4], axis=-1, keepdims=True)) + lam_init)

    r_i = (lax.broadcasted_iota(jnp.int32, (2 * HEAD_W, HEAD_W), 0) // ATTN_DH) & 1
    c_i = lax.broadcasted_iota(jnp.int32, (2 * HEAD_W, HEAD_W), 1) // ATTN_DH
    half_ones = jnp.where(r_i == c_i, 1.0, 0.0).astype(BF16)

    first_half = lax.broadcasted_iota(jnp.int32, (lq, HEAD_W), 1) < ATTN_DH
    sub_gain = sg_ref[...] * (1.0 - lam_init)
    ones = jnp.ones((lq + lctx, ATTN_VD), BF16)

    def stage_ctx_values(va_scr, hh):
        head = pl.program_id(1) * hps + hh
        rows = cv_ref[0, pl.ds(head, lctx, stride=ATTN_HEADS), :]
        va_scr[lq:lq + lctx, 0:ATTN_VD] = rows.astype(BF16)

    for hh in range(hps):
        cols = slice(hh * HEAD_W, (hh + 1) * HEAD_W)
        qn = _half_norm(q_ref[:, cols].astype(F32), qg_ref[...], half_ones)
        kn = _half_norm(k_ref[:, cols].astype(F32), kg_ref[...], half_ones)
        v = v_ref[:, cols]
        if emit_kv:
            knT_ref[0, hh] = kn.T
            head = pl.program_id(1) * hps + hh
            vo_ref[0, pl.ds(head, lq, stride=ATTN_HEADS), :] = v.astype(F32)
        if rope:
            qn = _rope(qn, cos_ref[...], sup_ref[...], sdn_ref[...])
            kn = _rope(kn, cos_ref[...], sup_ref[...], sdn_ref[...])
        qs = qn * (ATTN_DH ** -0.5 * LOG2E)
        q1_scr, q2_scr, kk_scr, va_scr = head_scr[hh]
        q1_scr[...] = jnp.where(first_half, qs, 0.0).astype(BF16)
        q2_scr[...] = jnp.where(first_half, 0.0, qs).astype(BF16)
        kk_scr[...] = kn.astype(BF16)
        va_scr[0:lq, 0:ATTN_VD] = v.astype(BF16)
        va_scr[:, ATTN_VD:2 * ATTN_VD] = ones
        if lctx:
            kcT_scrs[hh][...] = ckT_ref[0, hh].astype(BF16)
            stage_ctx_values(va_scr, hh)

        def softmax_pv(qb, kk_scr=kk_scr, va_scr=va_scr, hh=hh):
            s = _dot_nt(qb, kk_scr[...])
            if lctx:
                s = jnp.concatenate([s, _dot(qb, kcT_scrs[hh][...])], axis=1)
            p = jnp.exp2(s - jnp.max(s, axis=-1, keepdims=True)).astype(BF16)
            a = _dot(p, va_scr[...])
            return a[:, 0:ATTN_VD] / a[:, ATTN_VD:2 * ATTN_VD]

        o = softmax_pv(q1_scr[...]) - lam * softmax_pv(q2_scr[...])
        o = o * lax.rsqrt(jnp.mean(o * o, axis=-1, keepdims=True) + EPS) * sub_gain
        o_ref[:, cols] = o.astype(o_ref.dtype)


def _attn(p2d, lamp, qg, kg, sg, batch, lq, lam_init, hps, rope_tabs=None, ctx=None, emit_kv=False):
    m = p2d.shape[0]
    lctx = 0 if ctx is None else ctx[0].shape[-1]
    width = hps * HEAD_W
    qb, kb, vb = P_Q // width, P_K // width, P_V // width
    const = lambda b, h: (0, 0)
    in_specs = [pl.BlockSpec((4, ATTN_DH), const),
                pl.BlockSpec((1, HEAD_W), const),
                pl.BlockSpec((1, HEAD_W), const),
                pl.BlockSpec((1, ATTN_VD), const),
                pl.BlockSpec((lq, width), lambda b, h: (b, qb + h)),
                pl.BlockSpec((lq, width), lambda b, h: (b, kb + h)),
                pl.BlockSpec((lq, width), lambda b, h: (b, vb + h))]
    args = [lamp, qg, kg, sg, p2d, p2d, p2d]
    if rope_tabs is not None:
        in_specs += [pl.BlockSpec((lq, HEAD_W), const)] * 3
        args += list(rope_tabs)
    if ctx is not None:
        in_specs += [pl.BlockSpec((1, hps, HEAD_W, lctx), lambda b, h: (b, h, 0, 0)),
                     pl.BlockSpec((1, lctx * ATTN_HEADS, ATTN_VD), lambda b, h: (b, 0, 0))]
        args += list(ctx)
    head_out = pl.BlockSpec((lq, width), lambda b, h: (b, h))
    out_specs = [head_out]
    out_shape = [jax.ShapeDtypeStruct((m, D_MODEL), ACT_DTYPE)]
    if emit_kv:
        out_specs += [pl.BlockSpec((1, hps, HEAD_W, lq), lambda b, h: (b, h, 0, 0)),
                      pl.BlockSpec((1, lq * ATTN_HEADS, ATTN_VD), lambda b, h: (b, 0, 0))]
        out_shape += [jax.ShapeDtypeStruct((batch, ATTN_HEADS, HEAD_W, lq), F32),
                      jax.ShapeDtypeStruct((batch, lq * ATTN_HEADS, ATTN_VD), F32)]
    scratch = hps * [pltpu.VMEM((lq, HEAD_W), BF16),
                     pltpu.VMEM((lq, HEAD_W), BF16),
                     pltpu.VMEM((lq, HEAD_W), BF16),
                     pltpu.VMEM((lq + lctx, 2 * ATTN_VD), BF16)]
    if lctx:
        scratch += hps * [pltpu.VMEM((HEAD_W, lctx), BF16)]
    kern = functools.partial(_attn_kernel, lq=lq, lctx=lctx, hps=hps, lam_init=lam_init,
                             rope=rope_tabs is not None, emit_kv=emit_kv)
    return pl.pallas_call(
        kern,
        grid=(batch, ATTN_HEADS // hps),
        in_specs=in_specs,
        out_specs=out_specs,
        out_shape=out_shape,
        scratch_shapes=scratch,
        compiler_params=_cparams(("arbitrary", "arbitrary")),
        name="attn",
    )(*args)


def _ssd_kernel(*refs, seq, has_h0, gps):
    it = iter(refs)
    x_ref, b_ref, c_ref = next(it), next(it), next(it)
    cwx_ref, cwb_ref, cwc_ref = next(it), next(it), next(it)
    cbx_ref, cbb_ref, cbc_ref = next(it), next(it), next(it)
    dexp_ref, cs_ref, rtT_ref = next(it), next(it), next(it)
    if has_h0:
        h0f_ref, h0b_ref = next(it), next(it)
    y_ref, hf_ref, hb_ref = next(it), next(it), next(it)
    group_scr = [tuple(next(it) for _ in range(10)) for _ in range(gps)]

    g0 = pl.program_id(1) * gps
    nc = seq // CHUNK
    x_lanes = [slice(gi * GROUP_W, (gi + 1) * GROUP_W) for gi in range(gps)]
    n_lanes = [slice(gi * SSM_STATE, (gi + 1) * SSM_STATE) for gi in range(gps)]
    blk = lax.broadcasted_iota(jnp.int32, (CHUNK, GROUP_W), 1) // SSM_HEADDIM

    win_rows = CHUNK + 2 * CONV_EDGE
    sel_r = lax.broadcasted_iota(jnp.int32, (CHUNK, win_rows), 0)
    sel_c = lax.broadcasted_iota(jnp.int32, (CHUNK, win_rows), 1)
    taps = [d for d in range(-(D_CONV // 2), D_CONV // 2 + 1) if d != 0]
    shift_sel = {d: jnp.where(sel_c == sel_r + CONV_EDGE + d, 1.0, 0.0).astype(BF16) for d in taps}

    def conv_chunk(c, sources, shift_on_mxu):
        r0 = pl.multiple_of(c * CHUNK, CHUNK)
        p0 = pl.multiple_of(jnp.maximum(r0 - CONV_EDGE, 0), CONV_EDGE)
        n0 = pl.multiple_of(jnp.minimum(r0 + CHUNK, seq - CONV_EDGE), CONV_EDGE)

        def gather(rows_of):
            parts = [rows_of(*src) for src in sources]
            return parts[0] if len(parts) == 1 else jnp.concatenate(parts, axis=1)

        assert all(src[0].dtype == BF16 for src in sources)
        prev = gather(lambda ref, w_ref, b_ref, lanes: ref[pl.ds(p0, CONV_EDGE), lanes])
        nxt = gather(lambda ref, w_ref, b_ref, lanes: ref[pl.ds(n0, CONV_EDGE), lanes])
        main = gather(lambda ref, w_ref, b_ref, lanes: ref[pl.ds(r0, CHUNK), lanes])
        w = [gather(lambda ref, w_ref, b_ref, lanes, j=j: w_ref[j:j + 1, lanes]) for j in range(D_CONV)]
        bias = gather(lambda ref, w_ref, b_ref, lanes: b_ref[:, lanes])
        acc = bias + main.astype(F32) * w[D_CONV // 2]
        if shift_on_mxu:
            zero = jnp.zeros_like(prev)
            win = jnp.concatenate([jnp.where(c > 0, prev, zero), main,
                                   jnp.where(c < nc - 1, nxt, zero)], axis=0)
            for d in taps:
                acc = acc + _dot(shift_sel[d], win) * w[d + D_CONV // 2]
        else:
            rows = CHUNK + 2 * CONV_PAD
            win = jnp.concatenate(
                [jnp.where(c > 0, prev.astype(F32)[CONV_EDGE - CONV_PAD:, :], 0.0), main.astype(F32),
                 jnp.where(c < nc - 1, nxt.astype(F32)[:CONV_PAD, :], 0.0)], axis=0)
            for d in taps:
                tap = pltpu.roll(win, (-d) % rows, axis=0)[CONV_PAD:CONV_PAD + CHUNK, :]
                acc = acc + tap * w[d + D_CONV // 2]
        return _silu(acc)

    head_masks = [jnp.where(blk == j, 1.0, 0.0).astype(BF16) for j in range(HEADS_PER_GROUP)]

    def conv_body(c, carry):
        sl = pl.ds(pl.multiple_of(c * CHUNK, CHUNK), CHUNK)
        for gi in range(gps):
            xm_s, bT_s, cc_s, _, _, _, yacc, g_s, _, _ = group_scr[gi]
            xv = conv_chunk(c, [(x_ref, cwx_ref, cbx_ref, x_lanes[gi])], nc > 2)
            yacc[sl, :] = dexp_ref[:, x_lanes[gi]] * xv
            xb = xv.astype(BF16)
            for j in range(HEADS_PER_GROUP):
                xm_s[c, j * CHUNK:(j + 1) * CHUNK, :] = xb * head_masks[j]
            bc = conv_chunk(c, [(b_ref, cwb_ref, cbb_ref, n_lanes[gi]),
                                (c_ref, cwc_ref, cbc_ref, n_lanes[gi])], False)
            bT = bc[:, 0:SSM_STATE].T.astype(BF16)
            cc = bc[:, SSM_STATE:2 * SSM_STATE].astype(BF16)
            bT_s[c] = bT
            cc_s[sl, :] = cc
            g_s[c] = _dot(cc, bT)
        return carry

    lax.fori_loop(0, nc, conv_body, 0, unroll=2)

    for gi in range(gps):
        csg_s, stf, stb = group_scr[gi][3:6]
        shift = (LANES - HEADS_PER_GROUP * (g0 + gi)) % LANES
        csg_s[...] = pltpu.roll(cs_ref[...], shift, axis=1)
        if has_h0:
            stf[...] = h0f_ref[0, x_lanes[gi], :].T
            stb[...] = h0b_ref[0, x_lanes[gi], :].T
        else:
            stf[...] = jnp.zeros_like(stf)
            stb[...] = jnp.zeros_like(stb)

    row = lax.broadcasted_iota(jnp.int32, (CHUNK, CHUNK), 0)
    col = lax.broadcasted_iota(jnp.int32, (CHUNK, CHUNK), 1)
    lane_lo = lax.broadcasted_iota(jnp.int32, (CHUNK, LANES), 1) < SSM_HEADDIM

    def decay_terms(c, gi):
        xm_s, bT_s, cc_s, csg_s, _, _, yacc, g_s, upd_s, e_s = group_scr[gi]
        g = g0 + gi
        sl = pl.ds(pl.multiple_of(c * CHUNK, CHUNK), CHUNK)
        bT = bT_s[c]
        cs = csg_s[sl, :]
        x_heads = xm_s[c]
        dec_sum = [None] * HEADS_PER_GROUP
        for d, (lane0, mask, edge) in enumerate(((0, col <= row, CHUNK - 1), (SSM_HEADS, col >= row, 0))):
            upd_lhs, cols = [], []
            for j in range(HEADS_PER_GROUP):
                cb = jnp.broadcast_to(cs[:, lane0 + j:lane0 + j + 1], (CHUNK, CHUNK))
                rrow = rtT_ref[c, pl.ds(lane0 + HEADS_PER_GROUP * g + j, 1), :]
                dec = jnp.exp2(jnp.where(mask, cb - rrow, -jnp.inf))
                dec_sum[j] = dec if dec_sum[j] is None else dec_sum[j] + dec
                w_edge = dec[edge:edge + 1, :]
                upd_lhs.append(bT * w_edge.astype(BF16))
                cols.append(cb)
            upd_s[d, c] = _dot(jnp.concatenate(upd_lhs, axis=1), x_heads)
            cs_e = jnp.concatenate([jnp.where(lane_lo, cols[2 * t], cols[2 * t + 1])
                                    for t in range(HEADS_PER_GROUP // 2)], axis=1)
            e_s[d, c] = jnp.exp2(cs_e)
        gmat = g_s[c]
        diag_lhs = [(gmat * dec_sum[j]).astype(BF16) for j in range(HEADS_PER_GROUP)]
        yacc[sl, :] = yacc[sl, :] + _dot(jnp.concatenate(diag_lhs, axis=1), x_heads)

    def decay_body(c, carry):
        for gi in range(gps):
            decay_terms(c, gi)
        return carry

    lax.fori_loop(0, nc, decay_body, 0, unroll=2)

    def state_step(c, d, gi):
        _, _, cc_s, _, stf, stb, yacc, _, upd_s, e_s = group_scr[gi]
        st_ref = stb if d else stf
        edge = 0 if d else CHUNK - 1
        sl = pl.ds(pl.multiple_of(c * CHUNK, CHUNK), CHUNK)
        e = e_s[d, c]
        st = st_ref[...]
        yacc[sl, :] = yacc[sl, :] + e * _dot(cc_s[sl, :], st.astype(BF16))
        st_ref[...] = e[edge:edge + 1, :] * st + upd_s[d, c]

    def state_body(i, carry):
        for gi in range(gps):
            state_step(i, 0, gi)
            state_step(nc - 1 - i, 1, gi)
        return carry

    lax.fori_loop(0, nc, state_body, 0, unroll=2)

    for gi in range(gps):
        stf, stb, yacc = group_scr[gi][4:7]
        y_ref[:, x_lanes[gi]] = yacc[...].astype(y_ref.dtype)
        hf_ref[0, x_lanes[gi], :] = stf[...].T
        hb_ref[0, x_lanes[gi], :] = stb[...].T


def _ssd(p2d, cs, rtT, conv_w, conv_b, dexp, batch, seq, gps, h0=None):
    m = p2d.shape[0]
    nc = seq // CHUNK
    xw, nw = gps * GROUP_W, gps * SSM_STATE
    xb, bb, cb = P_X // xw, P_B // nw, P_C // nw
    cw_b0 = D_INNER // nw
    cw_c0 = cw_b0 + SSM_GROUPS // gps
    in_specs = [pl.BlockSpec((seq, xw), lambda b, g: (b, xb + g)),
                pl.BlockSpec((seq, nw), lambda b, g: (b, bb + g)),
                pl.BlockSpec((seq, nw), lambda b, g: (b, cb + g)),
                pl.BlockSpec((D_CONV, xw), lambda b, g: (0, g)),
                pl.BlockSpec((D_CONV, nw), lambda b, g: (0, cw_b0 + g)),
                pl.BlockSpec((D_CONV, nw), lambda b, g: (0, cw_c0 + g)),
                pl.BlockSpec((1, xw), lambda b, g: (0, g)),
                pl.BlockSpec((1, nw), lambda b, g: (0, cw_b0 + g)),
                pl.BlockSpec((1, nw), lambda b, g: (0, cw_c0 + g)),
                pl.BlockSpec((1, xw), lambda b, g: (0, g)),
                pl.BlockSpec((seq, LANES), lambda b, g: (b, 0)),
                pl.BlockSpec((nc, CHUNK, CHUNK), lambda b, g: (b, 0, 0))]
    args = [p2d, p2d, p2d, conv_w, conv_w, conv_w, conv_b, conv_b, conv_b, dexp, cs, rtT]
    state_spec = pl.BlockSpec((1, xw, SSM_STATE), lambda b, g: (b, g, 0))
    if h0 is not None:
        in_specs += [state_spec, state_spec]
        args += list(h0)
    state_shape = jax.ShapeDtypeStruct((batch, D_INNER, SSM_STATE), F32)
    kern = functools.partial(_ssd_kernel, seq=seq, has_h0=h0 is not None, gps=gps)
    return pl.pallas_call(
        kern,
        grid=(batch, SSM_GROUPS // gps),
        in_specs=in_specs,
        out_specs=[pl.BlockSpec((seq, xw), lambda b, g: (b, g)), state_spec, state_spec],
        out_shape=[jax.ShapeDtypeStruct((m, D_INNER), ACT_DTYPE), state_shape, state_shape],
        scratch_shapes=gps * [pltpu.VMEM((nc, HEADS_PER_GROUP * CHUNK, GROUP_W), BF16),
                              pltpu.VMEM((nc, SSM_STATE, CHUNK), BF16),
                              pltpu.VMEM((seq, SSM_STATE), BF16),
                              pltpu.VMEM((seq, LANES), F32),
                              pltpu.VMEM((SSM_STATE, GROUP_W), F32),
                              pltpu.VMEM((SSM_STATE, GROUP_W), F32),
                              pltpu.VMEM((seq, GROUP_W), F32),
                              pltpu.VMEM((nc, CHUNK, CHUNK), F32),
                              pltpu.VMEM((2, nc, SSM_STATE, GROUP_W), F32),
                              pltpu.VMEM((2, nc, CHUNK, GROUP_W), F32)],
        compiler_params=_cparams(("arbitrary", "arbitrary")),
        name="ssd",
    )(*args)


def _merge_kernel(x_ref, mod_ref, o_ref, y_ref, z_ref, ga_ref, gb_ref, sg_ref,
                  wa_ref, wb_ref, wo_ref, out_ref):
    out_a = _dot(o_ref[...], wa_ref[...])
    yz = y_ref[...].astype(F32) * _silu(z_ref[...].astype(F32))
    yn = yz * lax.rsqrt(jnp.mean(yz * yz, axis=-1, keepdims=True) + EPS) * sg_ref[...]
    out_b = _dot(yn.astype(BF16), wb_ref[...])
    merged = (jax.nn.sigmoid(ga_ref[...].astype(F32)) * out_a
              + jax.nn.sigmoid(gb_ref[...].astype(F32)) * out_b)
    gate1 = mod_ref[0, :, 2 * D_MODEL:3 * D_MODEL]
    out_ref[...] = x_ref[...] + gate1 * _dot(merged.astype(BF16), wo_ref[...])


def _merge(x2d, mod, o, y, p2d, ssm_g, wa, wb, wo, tm):
    m = x2d.shape[0]
    rows_per_mod = m // mod.shape[0]
    return pl.pallas_call(
        _merge_kernel,
        grid=(m // tm,),
        in_specs=[pl.BlockSpec((tm, D_MODEL), lambda i: (i, 0)),
                  pl.BlockSpec((1, 1, 6 * D_MODEL), lambda i: ((i * tm) // rows_per_mod, 0, 0)),
                  pl.BlockSpec((tm, D_MODEL), lambda i: (i, 0)),
                  pl.BlockSpec((tm, D_INNER), lambda i: (i, 0)),
                  pl.BlockSpec((tm, D_INNER), lambda i: (i, P_Z // D_INNER)),
                  pl.BlockSpec((tm, D_MODEL), lambda i: (i, P_GA // D_MODEL)),
                  pl.BlockSpec((tm, D_MODEL), lambda i: (i, P_GB // D_MODEL)),
                  _resident((1, D_INNER)),
                  _resident((D_MODEL, D_MODEL)),
                  _resident((D_INNER, D_MODEL)),
                  _resident((D_MODEL, D_MODEL))],
        out_specs=pl.BlockSpec((tm, D_MODEL), lambda i: (i, 0)),
        out_shape=jax.ShapeDtypeStruct((m, D_MODEL), F32),
        compiler_params=_cparams(("arbitrary",)),
        name="merge",
    )(x2d, mod, o, y, p2d, p2d, p2d, ssm_g, wa, wb, wo)


def _ffn_kernel(x_ref, mod_ref, g_ref, wg_ref, wu_ref, wd_ref, out_ref):
    x = x_ref[...]
    y = x * lax.rsqrt(jnp.mean(x * x, axis=-1, keepdims=True) + EPS) * g_ref[...]
    shift = mod_ref[0, :, 3 * D_MODEL:4 * D_MODEL]
    scale = mod_ref[0, :, 4 * D_MODEL:5 * D_MODEL]
    gate = mod_ref[0, :, 5 * D_MODEL:6 * D_MODEL]
    h = (y * (1.0 + scale) + shift).astype(BF16)
    f = (_silu(_dot(h, wg_ref[...])) * _dot(h, wu_ref[...])).astype(BF16)
    out_ref[...] = x + gate * _dot(f, wd_ref[...])


def _ffn(x2d, mod, g2, wg, wu, wd, tm):
    m = x2d.shape[0]
    rows_per_mod = m // mod.shape[0]
    return pl.pallas_call(
        _ffn_kernel,
        grid=(m // tm,),
        in_specs=[pl.BlockSpec((tm, D_MODEL), lambda i: (i, 0)),
                  pl.BlockSpec((1, 1, 6 * D_MODEL), lambda i: ((i * tm) // rows_per_mod, 0, 0)),
                  _resident((1, D_MODEL)),
                  _resident((D_MODEL, D_FF)),
                  _resident((D_MODEL, D_FF)),
                  _resident((D_FF, D_MODEL))],
        out_specs=pl.BlockSpec((tm, D_MODEL), lambda i: (i, 0)),
        out_shape=jax.ShapeDtypeStruct((m, D_MODEL), F32),
        compiler_params=_cparams(("arbitrary",)),
        name="ffn",
    )(x2d, mod, g2, wg, wu, wd)


NORM_ROWS = 1024
PROJ_ROWS = 4096
MIX_ROWS = 512
ATTN_STEP_ROWS = 2048
ATTN_STEP_ROWS_CTX = 4096


def _tiles(rows, n_mod, seq, has_ctx):
    rows_per_mod = rows // n_mod
    attn_rows = ATTN_STEP_ROWS_CTX if has_ctx else ATTN_STEP_ROWS
    return dict(
        norm_rows=min(NORM_ROWS, rows_per_mod),
        proj_rows=min(PROJ_ROWS, rows),
        mix_rows=min(MIX_ROWS, rows_per_mod),
        heads_per_step=max(1, min(ATTN_HEADS, attn_rows // seq)),
        groups_per_step=SSD_GROUPS_PER_STEP,
    )


def _rope_tables(n_tok):
    pos = np.arange(n_tok)
    row = (pos // GRID_W).astype(np.float32)
    colp = (pos % GRID_W).astype(np.float32)
    inv = (ROPE_BASE ** (-np.arange(0, ROPE_AXIS_DIM, 2, dtype=np.float32) / ROPE_AXIS_DIM)).astype(np.float32)
    ang_r = row[:, None] * inv[None, :]
    ang_c = colp[:, None] * inv[None, :]
    zero = np.zeros_like(ang_r)
    cos = np.concatenate([np.cos(ang_r)] * 2 + [np.cos(ang_c)] * 2, axis=-1)
    sup = np.concatenate([-np.sin(ang_r), zero, -np.sin(ang_c), zero], axis=-1)
    sdn = np.concatenate([zero, np.sin(ang_r), zero, np.sin(ang_c)], axis=-1)
    return tuple(np.tile(t, (1, 2)).astype(np.float32) for t in (cos, sup, sdn))


def _pad_lanes(v):
    flat = v.reshape(1, -1).astype(F32)
    return jnp.pad(flat, ((0, 0), (0, LANES - flat.shape[1])))


def kernel(x_prompt, x_sample, c, cache_k, cache_v, state_ssm_fwd, state_ssm_bwd, c_ctx, norm1_g, norm2_g, w_ada, b_ada, w_in, q_norm_g, k_norm_g, lambda_q1, lambda_k1, lambda_q2, lambda_k2, attn_sub_g, conv_w, conv_b, A_log, dt_bias, D_skip, ssm_norm_g, w_branch_a, w_branch_b, w_out, w_ffn_gate, w_ffn_up, w_ffn_down):
    depth = norm1_g.shape[0]
    assert depth == 1, "single-layer kernel"
    l = 0
    lam_init = 0.8 - 0.6 * math.exp(-0.3 * l)
    nb, seq, _ = x_prompt.shape
    db, dseq, _ = x_sample.shape

    cond = jnp.concatenate([c_ctx[None, :], c, jnp.zeros((SUBLANES - 1 - db, D_MODEL), F32)], axis=0)
    mod = _ada(cond, w_ada[l], b_ada[l][None, :])
    mod_ctx = mod[0:1].reshape(1, 1, 6 * D_MODEL)
    mod_lat = mod[1:1 + db].reshape(db, 1, 6 * D_MODEL)

    w = jnp.transpose(w_in[l])
    w_dt = jnp.pad(w[W_DT0:W_G0], ((0, LANES - 2 * SSM_HEADS), (0, 0)))
    wa = w_branch_a[l].astype(BF16)
    wb = w_branch_b[l].astype(BF16)
    wo = w_out[l].astype(BF16)
    wg = w_ffn_gate[l].astype(BF16)
    wu = w_ffn_up[l].astype(BF16)
    wd = w_ffn_down[l].astype(BF16)
    g1 = norm1_g[l][None, :]
    g2 = norm2_g[l][None, :]
    qg = jnp.tile(q_norm_g[l], 2)[None, :]
    kg = jnp.tile(k_norm_g[l], 2)[None, :]
    sg = attn_sub_g[l][None, :]
    lamp = jnp.stack([lambda_q1[l], lambda_k1[l], lambda_q2[l], lambda_k2[l]], axis=0)
    bias_row = _pad_lanes(dt_bias[l])
    alog_row = _pad_lanes(A_log[l])
    dexp = jnp.repeat(D_skip[l], SSM_HEADDIM)[None, :]
    cw = conv_w[l]
    cb = conv_b[l][None, :]
    ssm_g = ssm_norm_g[l][None, :]

    def layer(x, mod_rows, rope_tabs, ctx, h0, emit_kv):
        batch, sl, _ = x.shape
        x2d = x.reshape(batch * sl, D_MODEL)
        tiles = _tiles(batch * sl, mod_rows.shape[0], sl, ctx is not None)
        h, dt_raw = _norm1(x2d, mod_rows, g1, w_dt, tm=tiles["norm_rows"])
        p2d = _proj(h, w, tm=tiles["proj_rows"])
        cs, rtT = _dtprep(dt_raw, bias_row, alog_row)
        attn_out = _attn(p2d, lamp, qg, kg, sg, batch, sl, lam_init, tiles["heads_per_step"],
                         rope_tabs, ctx, emit_kv)
        y, hf, hb = _ssd(p2d, cs, rtT, cw, cb, dexp, batch, sl, tiles["groups_per_step"], h0)
        x1 = _merge(x2d, mod_rows, attn_out[0], y, p2d, ssm_g, wa, wb, wo, tm=tiles["mix_rows"])
        x2 = _ffn(x1, mod_rows, g2, wg, wu, wd, tm=tiles["mix_rows"])
        return x2.reshape(batch, sl, D_MODEL), attn_out[1:], hf, hb

    y_prompt, kv, hf, hb = layer(x_prompt, mod_ctx, None, None, None, True)
    ctx_kT = jnp.transpose(cache_k[:, l], (0, 2, 3, 4, 1)).reshape(db, ATTN_HEADS, HEAD_W, -1)
    ctx_v = cache_v[:, l].reshape(db, -1, ATTN_VD)
    h0 = (state_ssm_fwd[:, l].reshape(db, D_INNER, SSM_STATE),
          state_ssm_bwd[:, l].reshape(db, D_INNER, SSM_STATE))
    y_sample, _, _, _ = layer(x_sample, mod_lat, _rope_tables(dseq), (ctx_kT, ctx_v), h0, False)

    knT = kv[0].reshape(nb, ATTN_HEADS, 2, ATTN_DH, seq)
    new_cache_k = jnp.transpose(knT, (0, 4, 1, 2, 3)).reshape(nb, 1, seq, ATTN_HEADS, 2, ATTN_DH)
    new_cache_v = kv[1].reshape(nb, 1, seq, ATTN_HEADS, ATTN_VD)
    new_hf = hf.reshape(nb, 1, SSM_HEADS, SSM_HEADDIM, SSM_STATE)
    new_hb = hb.reshape(nb, 1, SSM_HEADS, SSM_HEADDIM, SSM_STATE)
    return (y_prompt, y_sample, new_cache_k, new_cache_v, new_hf, new_hb)
```

```python
import functools
import math

import numpy as np
import jax
import jax.numpy as jnp
from jax import lax
from jax.experimental import pallas as pl
from jax.experimental.pallas import tpu as pltpu

F32 = jnp.float32
BF16 = jnp.bfloat16

D_MODEL = 1024
GRID_W = 64
ATTN_HEADS = 8
ATTN_DH = 64
ATTN_VD = 128
HEAD_W = 2 * ATTN_DH
ROPE_BASE = 10000.0
ROPE_AXIS_DIM = ATTN_DH // 2
ROPE_HALF = ROPE_AXIS_DIM // 2
D_INNER = 2 * D_MODEL
SSM_HEADDIM = 64
SSM_HEADS = D_INNER // SSM_HEADDIM
SSM_GROUPS = 8
HEADS_PER_GROUP = SSM_HEADS // SSM_GROUPS
GROUP_W = HEADS_PER_GROUP * SSM_HEADDIM
SSM_STATE = 128
D_CONV = 5
CONV_PAD = 8
CONV_EDGE = 16
CHUNK = 128
D_FF = ((8 * D_MODEL // 3 + 255) // 256) * 256
EPS = 1e-6
LANES = 128
SUBLANES = 8
LOG2E = math.log2(math.e)

P_Z = 0
P_GA = P_Z + D_INNER
P_GB = P_GA + D_MODEL
P_X = P_GB + D_MODEL
P_B = P_X + D_INNER
P_C = P_B + SSM_GROUPS * SSM_STATE
P_Q = P_C + SSM_GROUPS * SSM_STATE
P_K = P_Q + D_MODEL
P_V = P_K + D_MODEL
NP = P_V + D_MODEL
PROJ_TN = 512
ACT_DTYPE = BF16

W_Q0 = 0
W_Z0 = 3 * D_MODEL
W_XBC0 = W_Z0 + D_INNER
W_DT0 = W_XBC0 + D_INNER + 2 * SSM_GROUPS * SSM_STATE
W_G0 = W_DT0 + 2 * SSM_HEADS

VMEM_LIMIT = 56 * 1024 * 1024


def _cparams(sem):
    return pltpu.CompilerParams(dimension_semantics=sem, vmem_limit_bytes=VMEM_LIMIT)


def _silu(x):
    return x * jax.nn.sigmoid(x)


def _dot(a, b):
    return jnp.dot(a, b, preferred_element_type=F32)


def _dot_nt(a, b):
    return lax.dot_general(a, b, (((1,), (1,)), ((), ())), preferred_element_type=F32)


def _split3(x):
    hi = x.astype(BF16)
    r = x - hi.astype(F32)
    mid = r.astype(BF16)
    lo = (r - mid.astype(F32)).astype(BF16)
    return hi, mid, lo


def _resident(shape):
    return pl.BlockSpec(shape, lambda *_: (0,) * len(shape), pipeline_mode=pl.Buffered(1))


def _ada_kernel(cond_ref, w_ref, b_ref, o_ref):
    s = _silu(cond_ref[...]).astype(BF16)
    o_ref[...] = _dot(s, w_ref[...].astype(BF16)) + b_ref[...]


def _ada(cond, w_ada, b_ada):
    rows = cond.shape[0]
    n = w_ada.shape[1]
    tn = 2048
    return pl.pallas_call(
        _ada_kernel,
        grid=(n // tn,),
        in_specs=[pl.BlockSpec((rows, D_MODEL), lambda j: (0, 0)),
                  pl.BlockSpec((D_MODEL, tn), lambda j: (0, j)),
                  pl.BlockSpec((1, tn), lambda j: (0, j))],
        out_specs=pl.BlockSpec((rows, tn), lambda j: (0, j)),
        out_shape=jax.ShapeDtypeStruct((rows, n), F32),
        compiler_params=_cparams(("arbitrary",)),
        name="ada",
    )(cond, w_ada, b_ada)


def _norm1_kernel(x_ref, mod_ref, g_ref, wdt_ref, h_ref, dt_ref):
    x = x_ref[...]
    y = x * lax.rsqrt(jnp.mean(x * x, axis=-1, keepdims=True) + EPS) * g_ref[...]
    shift = mod_ref[0, :, 0:D_MODEL]
    scale = mod_ref[0, :, D_MODEL:2 * D_MODEL]
    h = (y * (1.0 + scale) + shift).astype(BF16)
    h_ref[...] = h
    dt_ref[...] = _dot_nt(h, wdt_ref[...].astype(BF16))


def _norm1(x2d, mod, g1, w_dt, tm):
    m = x2d.shape[0]
    rows_per_mod = m // mod.shape[0]
    return pl.pallas_call(
        _norm1_kernel,
        grid=(m // tm,),
        in_specs=[pl.BlockSpec((tm, D_MODEL), lambda i: (i, 0)),
                  pl.BlockSpec((1, 1, 6 * D_MODEL), lambda i: ((i * tm) // rows_per_mod, 0, 0)),
                  _resident((1, D_MODEL)),
                  _resident((LANES, D_MODEL))],
        out_specs=[pl.BlockSpec((tm, D_MODEL), lambda i: (i, 0)),
                   pl.BlockSpec((tm, LANES), lambda i: (i, 0))],
        out_shape=[jax.ShapeDtypeStruct((m, D_MODEL), BF16),
                   jax.ShapeDtypeStruct((m, LANES), F32)],
        compiler_params=_cparams(("arbitrary",)),
        name="norm1",
    )(x2d, mod, g1, w_dt)


_GATE_TILE0 = P_GA // PROJ_TN
_N_GATE_TILES = 2 * D_MODEL // PROJ_TN
_GATE_ROW_OFFSET = W_G0 % PROJ_TN


def _w_in_tile(j):
    z_tiles = D_INNER // PROJ_TN
    xbc_tile0 = (P_X // PROJ_TN)
    qkv_tile0 = (P_Q // PROJ_TN)
    z_src = jnp.minimum(j, z_tiles - 1) + W_Z0 // PROJ_TN
    xbc_src = j - xbc_tile0 + W_XBC0 // PROJ_TN
    qkv_src = j - qkv_tile0 + W_Q0 // PROJ_TN
    return jnp.where(j < xbc_tile0, z_src, jnp.where(j < qkv_tile0, xbc_src, qkv_src))


def _proj_kernel(h_ref, win_ref, wg_lo_ref, wg_hi_ref, p_ref, w_scr, *, tm):
    j = pl.program_id(0)
    i = pl.program_id(1)
    is_gate = jnp.logical_and(j >= _GATE_TILE0, j < _GATE_TILE0 + _N_GATE_TILES)

    @pl.when(jnp.logical_and(i == 0, is_gate))
    def _():
        w_scr[0:PROJ_TN - _GATE_ROW_OFFSET, :] = wg_lo_ref[_GATE_ROW_OFFSET:, :].astype(BF16)
        w_scr[PROJ_TN - _GATE_ROW_OFFSET:, :] = wg_hi_ref[0:_GATE_ROW_OFFSET, :].astype(BF16)

    @pl.when(jnp.logical_and(i == 0, jnp.logical_not(is_gate)))
    def _():
        w_scr[...] = win_ref[...].astype(BF16)

    r0 = pl.multiple_of(i * tm, tm)
    p_ref[...] = _dot_nt(h_ref[pl.ds(r0, tm), :], w_scr[...]).astype(p_ref.dtype)


def _proj(h, w_in_t, tm):
    m = h.shape[0]
    tn = PROJ_TN

    def gate_block(j):
        return W_G0 // tn + jnp.clip(j - _GATE_TILE0, 0, _N_GATE_TILES - 1)

    return pl.pallas_call(
        functools.partial(_proj_kernel, tm=tm),
        grid=(NP // tn, m // tm),
        in_specs=[_resident((m, D_MODEL)),
                  pl.BlockSpec((tn, D_MODEL), lambda j, i: (_w_in_tile(j), 0)),
                  pl.BlockSpec((tn, D_MODEL), lambda j, i: (gate_block(j), 0)),
                  pl.BlockSpec((tn, D_MODEL), lambda j, i: (gate_block(j) + 1, 0))],
        out_specs=pl.BlockSpec((tm, tn), lambda j, i: (i, j)),
        out_shape=jax.ShapeDtypeStruct((m, NP), ACT_DTYPE),
        scratch_shapes=[pltpu.VMEM((tn, D_MODEL), BF16)],
        compiler_params=_cparams(("arbitrary", "arbitrary")),
        name="proj",
    )(h, w_in_t, w_in_t, w_in_t)


DTPREP_CHUNKS = 16
SSD_GROUPS_PER_STEP = 4

def _dtprep_kernel(dtr_ref, bias_ref, alog_ref, cs_ref, rtT_ref):
    row = lax.broadcasted_iota(jnp.int32, (CHUNK, CHUNK), 0)
    col = lax.broadcasted_iota(jnp.int32, (CHUNK, CHUNK), 1)
    tril = jnp.where(col <= row, 1.0, 0.0).astype(BF16)
    triu = jnp.where(col >= row, 1.0, 0.0).astype(BF16)
    neg_a = -jnp.exp(alog_ref[...])
    for c in range(DTPREP_CHUNKS):
        sl = slice(c * CHUNK, (c + 1) * CHUNK)
        x = dtr_ref[sl, :] + bias_ref[...]
        dt = jnp.maximum(x, 0.0) + jnp.log1p(jnp.exp(-jnp.abs(x)))
        hi, mid, lo = _split3(dt * neg_a)
        cf = _dot(tril, hi) + _dot(tril, mid) + _dot(tril, lo)
        cb = _dot(triu, hi) + _dot(triu, mid) + _dot(triu, lo)
        cs = jnp.where(col < SSM_HEADS, cf, cb)
        cs_ref[sl, :] = cs * LOG2E
        rtT_ref[c] = ((cs - jnp.log(dt)) * LOG2E).T


def _dtprep(dt_raw, bias_row, alog_row):
    m = dt_raw.shape[0]
    rows = DTPREP_CHUNKS * CHUNK
    return pl.pallas_call(
        _dtprep_kernel,
        grid=(m // rows,),
        in_specs=[pl.BlockSpec((rows, LANES), lambda i: (i, 0)),
                  _resident((1, LANES)),
                  _resident((1, LANES))],
        out_specs=[pl.BlockSpec((rows, LANES), lambda i: (i, 0)),
                   pl.BlockSpec((DTPREP_CHUNKS, CHUNK, CHUNK), lambda i: (i, 0, 0))],
        out_shape=[jax.ShapeDtypeStruct((m, LANES), F32),
                   jax.ShapeDtypeStruct((m // CHUNK, CHUNK, CHUNK), F32)],
        compiler_params=_cparams(("arbitrary",)),
        name="dtprep",
    )(dt_raw, bias_row, alog_row)


def _half_norm(x, g, half_ones):
    ss = x * x
    hi = ss.astype(BF16)
    lo = (ss - hi.astype(F32)).astype(BF16)
    ms = _dot(jnp.concatenate([hi, lo], axis=1), half_ones) * (1.0 / ATTN_DH)
    return x * lax.rsqrt(ms + EPS) * g


def _rope(x, cos, sin_up, sin_dn):
    x_up = pltpu.roll(x, HEAD_W - ROPE_HALF, axis=1)
    x_dn = pltpu.roll(x, ROPE_HALF, axis=1)
    return x * cos + x_up * sin_up + x_dn * sin_dn


def _attn_kernel(*refs, lq, lctx, hps, lam_init, rope, emit_kv):
    it = iter(refs)
    lamp_ref, qg_ref, kg_ref, sg_ref, q_ref, k_ref, v_ref = (next(it) for _ in range(7))
    if rope:
        cos_ref, sup_ref, sdn_ref = next(it), next(it), next(it)
    if lctx:
        ckT_ref, cv_ref = next(it), next(it)
    o_ref = next(it)
    if emit_kv:
        knT_ref, vo_ref = next(it), next(it)
    head_scr = [tuple(next(it) for _ in range(4)) for _ in range(hps)]
    if lctx:
        kcT_scrs = [next(it) for _ in range(hps)]

    lp = lamp_ref[...]
    lam = (jnp.exp(jnp.sum(lp[0:1] * lp[1:2], axis=-1, keepdims=True))
           - jnp.exp(jnp.sum(lp[2:3] * lp[3:4], axis=-1, keepdims=True)) + lam_init)

    r_i = (lax.broadcasted_iota(jnp.int32, (2 * HEAD_W, HEAD_W), 0) // ATTN_DH) & 1
    c_i = lax.broadcasted_iota(jnp.int32, (2 * HEAD_W, HEAD_W), 1) // ATTN_DH
    half_ones = jnp.where(r_i == c_i, 1.0, 0.0).astype(BF16)

    first_half = lax.broadcasted_iota(jnp.int32, (lq, HEAD_W), 1) < ATTN_DH
    sub_gain = sg_ref[...] * (1.0 - lam_init)
    ones = jnp.ones((lq + lctx, ATTN_VD), BF16)

    def stage_ctx_values(va_scr, hh):
        head = pl.program_id(1) * hps + hh
        rows = cv_ref[0, pl.ds(head, lctx, stride=ATTN_HEADS), :]
        va_scr[lq:lq + lctx, 0:ATTN_VD] = rows.astype(BF16)

    for hh in range(hps):
        cols = slice(hh * HEAD_W, (hh + 1) * HEAD_W)
        qn = _half_norm(q_ref[:, cols].astype(F32), qg_ref[...], half_ones)
        kn = _half_norm(k_ref[:, cols].astype(F32), kg_ref[...], half_ones)
        v = v_ref[:, cols]
        if emit_kv:
            knT_ref[0, hh] = kn.T
            head = pl.program_id(1) * hps + hh
            vo_ref[0, pl.ds(head, lq, stride=ATTN_HEADS), :] = v.astype(F32)
        if rope:
            qn = _rope(qn, cos_ref[...], sup_ref[...], sdn_ref[...])
            kn = _rope(kn, cos_ref[...], sup_ref[...], sdn_ref[...])
        qs = qn * (ATTN_DH ** -0.5 * LOG2E)
        q1_scr, q2_scr, kk_scr, va_scr = head_scr[hh]
        q1_scr[...] = jnp.where(first_half, qs, 0.0).astype(BF16)
        q2_scr[...] = jnp.where(first_half, 0.0, qs).astype(BF16)
        kk_scr[...] = kn.astype(BF16)
        va_scr[0:lq, 0:ATTN_VD] = v.astype(BF16)
        va_scr[:, ATTN_VD:2 * ATTN_VD] = ones
        if lctx:
            kcT_scrs[hh][...] = ckT_ref[0, hh].astype(BF16)
            stage_ctx_values(va_scr, hh)

        def softmax_pv(qb, kk_scr=kk_scr, va_scr=va_scr, hh=hh):
            s = _dot_nt(qb, kk_scr[...])
            if lctx:
                s = jnp.concatenate([s, _dot(qb, kcT_scrs[hh][...])], axis=1)
            p = jnp.exp2(s - jnp.max(s, axis=-1, keepdims=True)).astype(BF16)
            a = _dot(p, va_scr[...])
            return a[:, 0:ATTN_VD] / a[:, ATTN_VD:2 * ATTN_VD]

        o = softmax_pv(q1_scr[...]) - lam * softmax_pv(q2_scr[...])
        o = o * lax.rsqrt(jnp.mean(o * o, axis=-1, keepdims=True) + EPS) * sub_gain
        o_ref[:, cols] = o.astype(o_ref.dtype)


def _attn(p2d, lamp, qg, kg, sg, batch, lq, lam_init, hps, rope_tabs=None, ctx=None, emit_kv=False):
    m = p2d.shape[0]
    lctx = 0 if ctx is None else ctx[0].shape[-1]
    width = hps * HEAD_W
    qb, kb, vb = P_Q // width, P_K // width, P_V // width
    const = lambda b, h: (0, 0)
    in_specs = [pl.BlockSpec((4, ATTN_DH), const),
                pl.BlockSpec((1, HEAD_W), const),
                pl.BlockSpec((1, HEAD_W), const),
                pl.BlockSpec((1, ATTN_VD), const),
                pl.BlockSpec((lq, width), lambda b, h: (b, qb + h)),
                pl.BlockSpec((lq, width), lambda b, h: (b, kb + h)),
                pl.BlockSpec((lq, width), lambda b, h: (b, vb + h))]
    args = [lamp, qg, kg, sg, p2d, p2d, p2d]
    if rope_tabs is not None:
        in_specs += [pl.BlockSpec((lq, HEAD_W), const)] * 3
        args += list(rope_tabs)
    if ctx is not None:
        in_specs += [pl.BlockSpec((1, hps, HEAD_W, lctx), lambda b, h: (b, h, 0, 0)),
                     pl.BlockSpec((1, lctx * ATTN_HEADS, ATTN_VD), lambda b, h: (b, 0, 0))]
        args += list(ctx)
    head_out = pl.BlockSpec((lq, width), lambda b, h: (b, h))
    out_specs = [head_out]
    out_shape = [jax.ShapeDtypeStruct((m, D_MODEL), ACT_DTYPE)]
    if emit_kv:
        out_specs += [pl.BlockSpec((1, hps, HEAD_W, lq), lambda b, h: (b, h, 0, 0)),
                      pl.BlockSpec((1, lq * ATTN_HEADS, ATTN_VD), lambda b, h: (b, 0, 0))]
        out_shape += [jax.ShapeDtypeStruct((batch, ATTN_HEADS, HEAD_W, lq), F32),
                      jax.ShapeDtypeStruct((batch, lq * ATTN_HEADS, ATTN_VD), F32)]
    scratch = hps * [pltpu.VMEM((lq, HEAD_W), BF16),
                     pltpu.VMEM((lq, HEAD_W), BF16),
                     pltpu.VMEM((lq, HEAD_W), BF16),
                     pltpu.VMEM((lq + lctx, 2 * ATTN_VD), BF16)]
    if lctx:
        scratch += hps * [pltpu.VMEM((HEAD_W, lctx), BF16)]
    kern = functools.partial(_attn_kernel, lq=lq, lctx=lctx, hps=hps, lam_init=lam_init,
                             rope=rope_tabs is not None, emit_kv=emit_kv)
    return pl.pallas_call(
        kern,
        grid=(batch, ATTN_HEADS // hps),
        in_specs=in_specs,
        out_specs=out_specs,
        out_shape=out_shape,
        scratch_shapes=scratch,
        compiler_params=_cparams(("arbitrary", "arbitrary")),
        name="attn",
    )(*args)


def _ssd_kernel(*refs, seq, has_h0, gps):
    it = iter(refs)
    x_ref, b_ref, c_ref = next(it), next(it), next(it)
    cwx_ref, cwb_ref, cwc_ref = next(it), next(it), next(it)
    cbx_ref, cbb_ref, cbc_ref = next(it), next(it), next(it)
    dexp_ref, cs_ref, rtT_ref = next(it), next(it), next(it)
    if has_h0:
        h0f_ref, h0b_ref = next(it), next(it)
    y_ref, hf_ref, hb_ref = next(it), next(it), next(it)
    group_scr = [tuple(next(it) for _ in range(10)) for _ in range(gps)]

    g0 = pl.program_id(1) * gps
    nc = seq // CHUNK
    x_lanes = [slice(gi * GROUP_W, (gi + 1) * GROUP_W) for gi in range(gps)]
    n_lanes = [slice(gi * SSM_STATE, (gi + 1) * SSM_STATE) for gi in range(gps)]
    blk = lax.broadcasted_iota(jnp.int32, (CHUNK, GROUP_W), 1) // SSM_HEADDIM

    win_rows = CHUNK + 2 * CONV_EDGE
    sel_r = lax.broadcasted_iota(jnp.int32, (CHUNK, win_rows), 0)
    sel_c = lax.broadcasted_iota(jnp.int32, (CHUNK, win_rows), 1)
    taps = [d for d in range(-(D_CONV // 2), D_CONV // 2 + 1) if d != 0]
    shift_sel = {d: jnp.where(sel_c == sel_r + CONV_EDGE + d, 1.0, 0.0).astype(BF16) for d in taps}

    def conv_chunk(c, sources, shift_on_mxu):
        r0 = pl.multiple_of(c * CHUNK, CHUNK)
        p0 = pl.multiple_of(jnp.maximum(r0 - CONV_EDGE, 0), CONV_EDGE)
        n0 = pl.multiple_of(jnp.minimum(r0 + CHUNK, seq - CONV_EDGE), CONV_EDGE)

        def gather(rows_of):
            parts = [rows_of(*src) for src in sources]
            return parts[0] if len(parts) == 1 else jnp.concatenate(parts, axis=1)

        assert all(src[0].dtype == BF16 for src in sources)
        prev = gather(lambda ref, w_ref, b_ref, lanes: ref[pl.ds(p0, CONV_EDGE), lanes])
        nxt = gather(lambda ref, w_ref, b_ref, lanes: ref[pl.ds(n0, CONV_EDGE), lanes])
        main = gather(lambda ref, w_ref, b_ref, lanes: ref[pl.ds(r0, CHUNK), lanes])
        w = [gather(lambda ref, w_ref, b_ref, lanes, j=j: w_ref[j:j + 1, lanes]) for j in range(D_CONV)]
        bias = gather(lambda ref, w_ref, b_ref, lanes: b_ref[:, lanes])
        acc = bias + main.astype(F32) * w[D_CONV // 2]
        if shift_on_mxu:
            zero = jnp.zeros_like(prev)
            win = jnp.concatenate([jnp.where(c > 0, prev, zero), main,
                                   jnp.where(c < nc - 1, nxt, zero)], axis=0)
            for d in taps:
                acc = acc + _dot(shift_sel[d], win) * w[d + D_CONV // 2]
        else:
            rows = CHUNK + 2 * CONV_PAD
            win = jnp.concatenate(
                [jnp.where(c > 0, prev.astype(F32)[CONV_EDGE - CONV_PAD:, :], 0.0), main.astype(F32),
                 jnp.where(c < nc - 1, nxt.astype(F32)[:CONV_PAD, :], 0.0)], axis=0)
            for d in taps:
                tap = pltpu.roll(win, (-d) % rows, axis=0)[CONV_PAD:CONV_PAD + CHUNK, :]
                acc = acc + tap * w[d + D_CONV // 2]
        return _silu(acc)

    head_masks = [jnp.where(blk == j, 1.0, 0.0).astype(BF16) for j in range(HEADS_PER_GROUP)]

    def conv_body(c, carry):
        sl = pl.ds(pl.multiple_of(c * CHUNK, CHUNK), CHUNK)
        for gi in range(gps):
            xm_s, bT_s, cc_s, _, _, _, yacc, g_s, _, _ = group_scr[gi]
            xv = conv_chunk(c, [(x_ref, cwx_ref, cbx_ref, x_lanes[gi])], nc > 2)
            yacc[sl, :] = dexp_ref[:, x_lanes[gi]] * xv
            xb = xv.astype(BF16)
            for j in range(HEADS_PER_GROUP):
                xm_s[c, j * CHUNK:(j + 1) * CHUNK, :] = xb * head_masks[j]
            bc = conv_chunk(c, [(b_ref, cwb_ref, cbb_ref, n_lanes[gi]),
                                (c_ref, cwc_ref, cbc_ref, n_lanes[gi])], False)
            bT = bc[:, 0:SSM_STATE].T.astype(BF16)
            cc = bc[:, SSM_STATE:2 * SSM_STATE].astype(BF16)
            bT_s[c] = bT
            cc_s[sl, :] = cc
            g_s[c] = _dot(cc, bT)
        return carry

    lax.fori_loop(0, nc, conv_body, 0, unroll=2)

    for gi in range(gps):
        csg_s, stf, stb = group_scr[gi][3:6]
        shift = (LANES - HEADS_PER_GROUP * (g0 + gi)) % LANES
        csg_s[...] = pltpu.roll(cs_ref[...], shift, axis=1)
        if has_h0:
            stf[...] = h0f_ref[0, x_lanes[gi], :].T
            stb[...] = h0b_ref[0, x_lanes[gi], :].T
        else:
            stf[...] = jnp.zeros_like(stf)
            stb[...] = jnp.zeros_like(stb)

    row = lax.broadcasted_iota(jnp.int32, (CHUNK, CHUNK), 0)
    col = lax.broadcasted_iota(jnp.int32, (CHUNK, CHUNK), 1)
    lane_lo = lax.broadcasted_iota(jnp.int32, (CHUNK, LANES), 1) < SSM_HEADDIM

    def decay_terms(c, gi):
        xm_s, bT_s, cc_s, csg_s, _, _, yacc, g_s, upd_s, e_s = group_scr[gi]
        g = g0 + gi
        sl = pl.ds(pl.multiple_of(c * CHUNK, CHUNK), CHUNK)
        bT = bT_s[c]
        cs = csg_s[sl, :]
        x_heads = xm_s[c]
        dec_sum = [None] * HEADS_PER_GROUP
        for d, (lane0, mask, edge) in enumerate(((0, col <= row, CHUNK - 1), (SSM_HEADS, col >= row, 0))):
            upd_lhs, cols = [], []
            for j in range(HEADS_PER_GROUP):
                cb = jnp.broadcast_to(cs[:, lane0 + j:lane0 + j + 1], (CHUNK, CHUNK))
                rrow = rtT_ref[c, pl.ds(lane0 + HEADS_PER_GROUP * g + j, 1), :]
                dec = jnp.exp2(jnp.where(mask, cb - rrow, -jnp.inf))
                dec_sum[j] = dec if dec_sum[j] is None else dec_sum[j] + dec
                w_edge = dec[edge:edge + 1, :]
                upd_lhs.append(bT * w_edge.astype(BF16))
                cols.append(cb)
            upd_s[d, c] = _dot(jnp.concatenate(upd_lhs, axis=1), x_heads)
            cs_e = jnp.concatenate([jnp.where(lane_lo, cols[2 * t], cols[2 * t + 1])
                                    for t in range(HEADS_PER_GROUP // 2)], axis=1)
            e_s[d, c] = jnp.exp2(cs_e)
        gmat = g_s[c]
        diag_lhs = [(gmat * dec_sum[j]).astype(BF16) for j in range(HEADS_PER_GROUP)]
        yacc[sl, :] = yacc[sl, :] + _dot(jnp.concatenate(diag_lhs, axis=1), x_heads)

    def decay_body(c, carry):
        for gi in range(gps):
            decay_terms(c, gi)
        return carry

    lax.fori_loop(0, nc, decay_body, 0, unroll=2)

    def state_step(c, d, gi):
        _, _, cc_s, _, stf, stb, yacc, _, upd_s, e_s = group_scr[gi]
        st_ref = stb if d else stf
        edge = 0 if d else CHUNK - 1
        sl = pl.ds(pl.multiple_of(c * CHUNK, CHUNK), CHUNK)
        e = e_s[d, c]
        st = st_ref[...]
        yacc[sl, :] = yacc[sl, :] + e * _dot(cc_s[sl, :], st.astype(BF16))
        st_ref[...] = e[edge:edge + 1, :] * st + upd_s[d, c]

    def state_body(i, carry):
        for gi in range(gps):
            state_step(i, 0, gi)
            state_step(nc - 1 - i, 1, gi)
        return carry

    lax.fori_loop(0, nc, state_body, 0, unroll=2)

    for gi in range(gps):
        stf, stb, yacc = group_scr[gi][4:7]
        y_ref[:, x_lanes[gi]] = yacc[...].astype(y_ref.dtype)
        hf_ref[0, x_lanes[gi], :] = stf[...].T
        hb_ref[0, x_lanes[gi], :] = stb[...].T


def _ssd(p2d, cs, rtT, conv_w, conv_b, dexp, batch, seq, gps, h0=None):
    m = p2d.shape[0]
    nc = seq // CHUNK
    xw, nw = gps * GROUP_W, gps * SSM_STATE
    xb, bb, cb = P_X // xw, P_B // nw, P_C // nw
    cw_b0 = D_INNER // nw
    cw_c0 = cw_b0 + SSM_GROUPS // gps
    in_specs = [pl.BlockSpec((seq, xw), lambda b, g: (b, xb + g)),
                pl.BlockSpec((seq, nw), lambda b, g: (b, bb + g)),
                pl.BlockSpec((seq, nw), lambda b, g: (b, cb + g)),
                pl.BlockSpec((D_CONV, xw), lambda b, g: (0, g)),
                pl.BlockSpec((D_CONV, nw), lambda b, g: (0, cw_b0 + g)),
                pl.BlockSpec((D_CONV, nw), lambda b, g: (0, cw_c0 + g)),
                pl.BlockSpec((1, xw), lambda b, g: (0, g)),
                pl.BlockSpec((1, nw), lambda b, g: (0, cw_b0 + g)),
                pl.BlockSpec((1, nw), lambda b, g: (0, cw_c0 + g)),
                pl.BlockSpec((1, xw), lambda b, g: (0, g)),
                pl.BlockSpec((seq, LANES), lambda b, g: (b, 0)),
                pl.BlockSpec((nc, CHUNK, CHUNK), lambda b, g: (b, 0, 0))]
    args = [p2d, p2d, p2d, conv_w, conv_w, conv_w, conv_b, conv_b, conv_b, dexp, cs, rtT]
    state_spec = pl.BlockSpec((1, xw, SSM_STATE), lambda b, g: (b, g, 0))
    if h0 is not None:
        in_specs += [state_spec, state_spec]
        args += list(h0)
    state_shape = jax.ShapeDtypeStruct((batch, D_INNER, SSM_STATE), F32)
    kern = functools.partial(_ssd_kernel, seq=seq, has_h0=h0 is not None, gps=gps)
    return pl.pallas_call(
        kern,
        grid=(batch, SSM_GROUPS // gps),
        in_specs=in_specs,
        out_specs=[pl.BlockSpec((seq, xw), lambda b, g: (b, g)), state_spec, state_spec],
        out_shape=[jax.ShapeDtypeStruct((m, D_INNER), ACT_DTYPE), state_shape, state_shape],
        scratch_shapes=gps * [pltpu.VMEM((nc, HEADS_PER_GROUP * CHUNK, GROUP_W), BF16),
                              pltpu.VMEM((nc, SSM_STATE, CHUNK), BF16),
                              pltpu.VMEM((seq, SSM_STATE), BF16),
                              pltpu.VMEM((seq, LANES), F32),
                              pltpu.VMEM((SSM_STATE, GROUP_W), F32),
                              pltpu.VMEM((SSM_STATE, GROUP_W), F32),
                              pltpu.VMEM((seq, GROUP_W), F32),
                              pltpu.VMEM((nc, CHUNK, CHUNK), F32),
                              pltpu.VMEM((2, nc, SSM_STATE, GROUP_W), F32),
                              pltpu.VMEM((2, nc, CHUNK, GROUP_W), F32)],
        compiler_params=_cparams(("arbitrary", "arbitrary")),
        name="ssd",
    )(*args)


def _merge_kernel(x_ref, mod_ref, o_ref, y_ref, z_ref, ga_ref, gb_ref, sg_ref,
                  wa_ref, wb_ref, wo_ref, out_ref):
    out_a = _dot(o_ref[...], wa_ref[...])
    yz = y_ref[...].astype(F32) * _silu(z_ref[...].astype(F32))
    yn = yz * lax.rsqrt(jnp.mean(yz * yz, axis=-1, keepdims=True) + EPS) * sg_ref[...]
    out_b = _dot(yn.astype(BF16), wb_ref[...])
    merged = (jax.nn.sigmoid(ga_ref[...].astype(F32)) * out_a
              + jax.nn.sigmoid(gb_ref[...].astype(F32)) * out_b)
    gate1 = mod_ref[0, :, 2 * D_MODEL:3 * D_MODEL]
    out_ref[...] = x_ref[...] + gate1 * _dot(merged.astype(BF16), wo_ref[...])


def _merge(x2d, mod, o, y, p2d, ssm_g, wa, wb, wo, tm):
    m = x2d.shape[0]
    rows_per_mod = m // mod.shape[0]
    return pl.pallas_call(
        _merge_kernel,
        grid=(m // tm,),
        in_specs=[pl.BlockSpec((tm, D_MODEL), lambda i: (i, 0)),
                  pl.BlockSpec((1, 1, 6 * D_MODEL), lambda i: ((i * tm) // rows_per_mod, 0, 0)),
                  pl.BlockSpec((tm, D_MODEL), lambda i: (i, 0)),
                  pl.BlockSpec((tm, D_INNER), lambda i: (i, 0)),
                  pl.BlockSpec((tm, D_INNER), lambda i: (i, P_Z // D_INNER)),
                  pl.BlockSpec((tm, D_MODEL), lambda i: (i, P_GA // D_MODEL)),
                  pl.BlockSpec((tm, D_MODEL), lambda i: (i, P_GB // D_MODEL)),
                  _resident((1, D_INNER)),
                  _resident((D_MODEL, D_MODEL)),
                  _resident((D_INNER, D_MODEL)),
                  _resident((D_MODEL, D_MODEL))],
        out_specs=pl.BlockSpec((tm, D_MODEL), lambda i: (i, 0)),
        out_shape=jax.ShapeDtypeStruct((m, D_MODEL), F32),
        compiler_params=_cparams(("arbitrary",)),
        name="merge",
    )(x2d, mod, o, y, p2d, p2d, p2d, ssm_g, wa, wb, wo)


def _ffn_kernel(x_ref, mod_ref, g_ref, wg_ref, wu_ref, wd_ref, out_ref):
    x = x_ref[...]
    y = x * lax.rsqrt(jnp.mean(x * x, axis=-1, keepdims=True) + EPS) * g_ref[...]
    shift = mod_ref[0, :, 3 * D_MODEL:4 * D_MODEL]
    scale = mod_ref[0, :, 4 * D_MODEL:5 * D_MODEL]
    gate = mod_ref[0, :, 5 * D_MODEL:6 * D_MODEL]
    h = (y * (1.0 + scale) + shift).astype(BF16)
    f = (_silu(_dot(h, wg_ref[...])) * _dot(h, wu_ref[...])).astype(BF16)
    out_ref[...] = x + gate * _dot(f, wd_ref[...])


def _ffn(x2d, mod, g2, wg, wu, wd, tm):
    m = x2d.shape[0]
    rows_per_mod = m // mod.shape[0]
    return pl.pallas_call(
        _ffn_kernel,
        grid=(m // tm,),
        in_specs=[pl.BlockSpec((tm, D_MODEL), lambda i: (i, 0)),
                  pl.BlockSpec((1, 1, 6 * D_MODEL), lambda i: ((i * tm) // rows_per_mod, 0, 0)),
                  _resident((1, D_MODEL)),
                  _resident((D_MODEL, D_FF)),
                  _resident((D_MODEL, D_FF)),
                  _resident((D_FF, D_MODEL))],
        out_specs=pl.BlockSpec((tm, D_MODEL), lambda i: (i, 0)),
        out_shape=jax.ShapeDtypeStruct((m, D_MODEL), F32),
        compiler_params=_cparams(("arbitrary",)),
        name="ffn",
    )(x2d, mod, g2, wg, wu, wd)


NORM_ROWS = 1024
PROJ_ROWS = 4096
MIX_ROWS = 512
ATTN_STEP_ROWS = 2048
ATTN_STEP_ROWS_CTX = 4096


def _tiles(rows, n_mod, seq, has_ctx):
    rows_per_mod = rows // n_mod
    attn_rows = ATTN_STEP_ROWS_CTX if has_ctx else ATTN_STEP_ROWS
    return dict(
        norm_rows=min(NORM_ROWS, rows_per_mod),
        proj_rows=min(PROJ_ROWS, rows),
        mix_rows=min(MIX_ROWS, rows_per_mod),
        heads_per_step=max(1, min(ATTN_HEADS, attn_rows // seq)),
        groups_per_step=SSD_GROUPS_PER_STEP,
    )


def _rope_tables(n_tok):
    pos = np.arange(n_tok)
    row = (pos // GRID_W).astype(np.float32)
    colp = (pos % GRID_W).astype(np.float32)
    inv = (ROPE_BASE ** (-np.arange(0, ROPE_AXIS_DIM, 2, dtype=np.float32) / ROPE_AXIS_DIM)).astype(np.float32)
    ang_r = row[:, None] * inv[None, :]
    ang_c = colp[:, None] * inv[None, :]
    zero = np.zeros_like(ang_r)
    cos = np.concatenate([np.cos(ang_r)] * 2 + [np.cos(ang_c)] * 2, axis=-1)
    sup = np.concatenate([-np.sin(ang_r), zero, -np.sin(ang_c), zero], axis=-1)
    sdn = np.concatenate([zero, np.sin(ang_r), zero, np.sin(ang_c)], axis=-1)
    return tuple(np.tile(t, (1, 2)).astype(np.float32) for t in (cos, sup, sdn))


def _pad_lanes(v):
    flat = v.reshape(1, -1).astype(F32)
    return jnp.pad(flat, ((0, 0), (0, LANES - flat.shape[1])))


def kernel(x_prompt, x_sample, c, cache_k, cache_v, state_ssm_fwd, state_ssm_bwd, c_ctx, norm1_g, norm2_g, w_ada, b_ada, w_in, q_norm_g, k_norm_g, lambda_q1, lambda_k1, lambda_q2, lambda_k2, attn_sub_g, conv_w, conv_b, A_log, dt_bias, D_skip, ssm_norm_g, w_branch_a, w_branch_b, w_out, w_ffn_gate, w_ffn_up, w_ffn_down):
    depth = norm1_g.shape[0]
    assert depth == 1, "single-layer kernel"
    l = 0
    lam_init = 0.8 - 0.6 * math.exp(-0.3 * l)
    nb, seq, _ = x_prompt.shape
    db, dseq, _ = x_sample.shape

    cond = jnp.concatenate([c_ctx[None, :], c, jnp.zeros((SUBLANES - 1 - db, D_MODEL), F32)], axis=0)
    mod = _ada(cond, w_ada[l], b_ada[l][None, :])
    mod_ctx = mod[0:1].reshape(1, 1, 6 * D_MODEL)
    mod_lat = mod[1:1 + db].reshape(db, 1, 6 * D_MODEL)

    w = jnp.transpose(w_in[l])
    w_dt = jnp.pad(w[W_DT0:W_G0], ((0, LANES - 2 * SSM_HEADS), (0, 0)))
    wa = w_branch_a[l].astype(BF16)
    wb = w_branch_b[l].astype(BF16)
    wo = w_out[l].astype(BF16)
    wg = w_ffn_gate[l].astype(BF16)
    wu = w_ffn_up[l].astype(BF16)
    wd = w_ffn_down[l].astype(BF16)
    g1 = norm1_g[l][None, :]
    g2 = norm2_g[l][None, :]
    qg = jnp.tile(q_norm_g[l], 2)[None, :]
    kg = jnp.tile(k_norm_g[l], 2)[None, :]
    sg = attn_sub_g[l][None, :]
    lamp = jnp.stack([lambda_q1[l], lambda_k1[l], lambda_q2[l], lambda_k2[l]], axis=0)
    bias_row = _pad_lanes(dt_bias[l])
    alog_row = _pad_lanes(A_log[l])
    dexp = jnp.repeat(D_skip[l], SSM_HEADDIM)[None, :]
    cw = conv_w[l]
    cb = conv_b[l][None, :]
    ssm_g = ssm_norm_g[l][None, :]

    def layer(x, mod_rows, rope_tabs, ctx, h0, emit_kv):
        batch, sl, _ = x.shape
        x2d = x.reshape(batch * sl, D_MODEL)
        tiles = _tiles(batch * sl, mod_rows.shape[0], sl, ctx is not None)
        h, dt_raw = _norm1(x2d, mod_rows, g1, w_dt, tm=tiles["norm_rows"])
        p2d = _proj(h, w, tm=tiles["proj_rows"])
        cs, rtT = _dtprep(dt_raw, bias_row, alog_row)
        attn_out = _attn(p2d, lamp, qg, kg, sg, batch, sl, lam_init, tiles["heads_per_step"],
                         rope_tabs, ctx, emit_kv)
        y, hf, hb = _ssd(p2d, cs, rtT, cw, cb, dexp, batch, sl, tiles["groups_per_step"], h0)
        x1 = _merge(x2d, mod_rows, attn_out[0], y, p2d, ssm_g, wa, wb, wo, tm=tiles["mix_rows"])
        x2 = _ffn(x1, mod_rows, g2, wg, wu, wd, tm=tiles["mix_rows"])
        return x2.reshape(batch, sl, D_MODEL), attn_out[1:], hf, hb

    y_prompt, kv, hf, hb = layer(x_prompt, mod_ctx, None, None, None, True)
    ctx_kT = jnp.transpose(cache_k[:, l], (0, 2, 3, 4, 1)).reshape(db, ATTN_HEADS, HEAD_W, -1)
    ctx_v = cache_v[:, l].reshape(db, -1, ATTN_VD)
    h0 = (state_ssm_fwd[:, l].reshape(db, D_INNER, SSM_STATE),
          state_ssm_bwd[:, l].reshape(db, D_INNER, SSM_STATE))
    y_sample, _, _, _ = layer(x_sample, mod_lat, _rope_tables(dseq), (ctx_kT, ctx_v), h0, False)

    knT = kv[0].reshape(nb, ATTN_HEADS, 2, ATTN_DH, seq)
    new_cache_k = jnp.transpose(knT, (0, 4, 1, 2, 3)).reshape(nb, 1, seq, ATTN_HEADS, 2, ATTN_DH)
    new_cache_v = kv[1].reshape(nb, 1, seq, ATTN_HEADS, ATTN_VD)
    new_hf = hf.reshape(nb, 1, SSM_HEADS, SSM_HEADDIM, SSM_STATE)
    new_hb = hb.reshape(nb, 1, SSM_HEADS, SSM_HEADDIM, SSM_STATE)
    return (y_prompt, y_sample, new_cache_k, new_cache_v, new_hf, new_hb)
```
